```python
import math
import jax, jax.numpy as jnp
from jax import lax
import numpy as np

D_MODEL = 1024
BATCH = 16
SEQ = 2048
DEPTH = 1

N_DIFF_HEADS = 4
DIFF_HEAD_DIM = 64
DIFF_VDIM = 2 * DIFF_HEAD_DIM
DIFF_WIDTH = N_DIFF_HEADS * DIFF_VDIM
SGU_WIDTH = D_MODEL - DIFF_WIDTH
SGU_GROUPS = 4
SGU_GROUP_DIM = SGU_WIDTH // SGU_GROUPS
CHUNK = 128
Q_BLOCK = 128
ROT_DIM = DIFF_HEAD_DIM // 4
ROPE_THETA = 500000.0
D_FF = ((-(-8 * D_MODEL // 3)) + 255) // 256 * 256
ALPHA = (2 * DEPTH) ** 0.25
BETA = (8 * DEPTH) ** -0.25
LN_EPS = 1e-5
QKV_COLS = DIFF_WIDTH
PROJ_COLS = 3 * QKV_COLS + 2 * SGU_WIDTH

kernel_name = "hybrid_diffattn_sgu_deepnorm_adaln"


def _layernorm(x, g, b):
    xf = x.astype(jnp.float32)
    mu = jnp.mean(xf, axis=-1, keepdims=True)
    var = jnp.mean(jnp.square(xf - mu), axis=-1, keepdims=True)
    y = (xf - mu) * lax.rsqrt(var + LN_EPS)
    return (y * g.astype(jnp.float32) + b.astype(jnp.float32)).astype(x.dtype)


def _rmsnorm(x, g):
    xf = x.astype(jnp.float32)
    y = xf * lax.rsqrt(jnp.mean(jnp.square(xf), axis=-1, keepdims=True) + LN_EPS)
    return (y * g.astype(jnp.float32)).astype(x.dtype)


def _rope_tables(seq):
    half = ROT_DIM // 2
    inv_freq = ROPE_THETA ** (-jnp.arange(half, dtype=jnp.float32) * 2.0 / ROT_DIM)
    ang = jnp.arange(seq, dtype=jnp.float32)[:, None] * inv_freq[None, :]
    return jnp.cos(ang), jnp.sin(ang)


def _partial_rope(t, cos, sin):
    half = ROT_DIM // 2
    cs = cos[None, :, None, :].astype(t.dtype)
    sn = sin[None, :, None, :].astype(t.dtype)
    t1, t2, tp = t[..., :half], t[..., half:ROT_DIM], t[..., ROT_DIM:]
    return jnp.concatenate([t1 * cs - t2 * sn, t1 * sn + t2 * cs, tp], axis=-1)


def _diff_attention(q, k, v, lam, lam_init, subln_g, cos, sin):
    B, S, _ = q.shape
    H, d = N_DIFF_HEADS, DIFF_HEAD_DIM
    q = _partial_rope(q.reshape(B, S, 2 * H, d), cos, sin)
    k = _partial_rope(k.reshape(B, S, 2 * H, d), cos, sin)
    q = (q * (d ** -0.5)).reshape(B, S, H, 2, d).transpose(0, 2, 3, 1, 4)
    k = k.reshape(B, S, H, 2, d).transpose(0, 2, 3, 1, 4)
    v = v.reshape(B, S, H, DIFF_VDIM).transpose(0, 2, 1, 3)
    n_blocks = S // Q_BLOCK
    kpos = jnp.arange(S)

    def block(i):
        qi = lax.dynamic_slice_in_dim(q, i * Q_BLOCK, Q_BLOCK, axis=3)
        s = jnp.einsum('bhcqd,bhckd->bhcqk', qi, k).astype(jnp.float32)
        qpos = i * Q_BLOCK + jnp.arange(Q_BLOCK)
        mask = kpos[None, :] <= qpos[:, None]
        s = jnp.where(mask, s, jnp.finfo(jnp.float32).min)
        p = jax.nn.softmax(s, axis=-1)
        a = (p[:, :, 0] - lam * p[:, :, 1]).astype(v.dtype)
        return jnp.einsum('bhqk,bhke->bhqe', a, v)

    o = lax.map(block, jnp.arange(n_blocks))
    o = o.transpose(1, 0, 3, 2, 4).reshape(B, S, H, DIFF_VDIM)
    o = _rmsnorm(o, subln_g) * (1.0 - lam_init)
    return o.reshape(B, S, DIFF_WIDTH)


def _spatial_gating(z, ln_g, ln_b, w_s, b_s):
    B, S, _ = z.shape
    u, v = z[..., :SGU_WIDTH], z[..., SGU_WIDTH:]
    v = _layernorm(v, ln_g, ln_b)
    v = v.reshape(B, S // CHUNK, CHUNK, SGU_GROUPS, SGU_GROUP_DIM)
    causal = jnp.tril(jnp.ones((CHUNK, CHUNK), dtype=bool))
    w = jnp.where(causal[None], w_s, jnp.zeros_like(w_s))
    sv = jnp.einsum('gts,bnsgc->bntgc', w, v) + b_s.T[None, None, :, :, None]
    return u * sv.reshape(B, S, SGU_WIDTH)


def _swiglu(h, w_gate, w_up, w_down):
    return (jax.nn.silu(h @ w_gate) * (h @ w_up)) @ w_down


def setup_inputs(seed: int = 0) -> dict:
    key = jax.random.key(seed)
    ks = jax.random.split(key, 24)
    D = D_MODEL
    f32 = jnp.float32
    nrm = lambda k, shape, s: (jax.random.normal(k, shape, f32) * s)
    x = nrm(ks[0], (BATCH, SEQ, D), 1.0)
    c = nrm(ks[1], (BATCH, D), 1.0)
    ada_w = nrm(ks[2], (DEPTH, D, 6 * D), D ** -0.5)
    ada_b = nrm(ks[3], (DEPTH, 6 * D), 0.02)
    w_qk = nrm(ks[4], (DEPTH, D, 2 * QKV_COLS), D ** -0.5)
    w_vs = nrm(ks[5], (DEPTH, D, QKV_COLS + 2 * SGU_WIDTH), D ** -0.5 * BETA)
    w_in = jnp.concatenate([w_qk, w_vs], axis=-1)
    lambda_q1 = nrm(ks[6], (DEPTH, DIFF_HEAD_DIM), 0.1)
    lambda_k1 = nrm(ks[7], (DEPTH, DIFF_HEAD_DIM), 0.1)
    lambda_q2 = nrm(ks[8], (DEPTH, DIFF_HEAD_DIM), 0.1)
    lambda_k2 = nrm(ks[9], (DEPTH, DIFF_HEAD_DIM), 0.1)
    subln_g = 1.0 + nrm(ks[10], (DEPTH, DIFF_VDIM), 0.02)
    sgu_ln_g = 1.0 + nrm(ks[11], (DEPTH, SGU_WIDTH), 0.02)
    sgu_ln_b = nrm(ks[12], (DEPTH, SGU_WIDTH), 0.02)
    sgu_w = nrm(ks[13], (DEPTH, SGU_GROUPS, CHUNK, CHUNK), CHUNK ** -0.5)
    sgu_b = 1.0 + nrm(ks[14], (DEPTH, SGU_GROUPS, CHUNK), 0.1)
    w_o = nrm(ks[15], (DEPTH, D, D), D ** -0.5 * BETA)
    ln1_g = 1.0 + nrm(ks[16], (DEPTH, D), 0.02)
    ln1_b = nrm(ks[17], (DEPTH, D), 0.02)
    w_gate = nrm(ks[18], (DEPTH, D, D_FF), D ** -0.5 * BETA)
    w_up = nrm(ks[19], (DEPTH, D, D_FF), D ** -0.5 * BETA)
    w_down = nrm(ks[20], (DEPTH, D_FF, D), D_FF ** -0.5 * BETA)
    ln2_g = 1.0 + nrm(ks[21], (DEPTH, D), 0.02)
    ln2_b = nrm(ks[22], (DEPTH, D), 0.02)
    return {"x": x, "c": c, "ada_w": ada_w, "ada_b": ada_b, "w_in": w_in,
            "lambda_q1": lambda_q1, "lambda_k1": lambda_k1, "lambda_q2": lambda_q2,
            "lambda_k2": lambda_k2, "subln_g": subln_g, "sgu_ln_g": sgu_ln_g,
            "sgu_ln_b": sgu_ln_b, "sgu_w": sgu_w, "sgu_b": sgu_b, "w_o": w_o,
            "ln1_g": ln1_g, "ln1_b": ln1_b, "w_gate": w_gate, "w_up": w_up,
            "w_down": w_down, "ln2_g": ln2_g, "ln2_b": ln2_b}


def reference(x, c, ada_w, ada_b, w_in, lambda_q1, lambda_k1, lambda_q2, lambda_k2,
              subln_g, sgu_ln_g, sgu_ln_b, sgu_w, sgu_b, w_o, ln1_g, ln1_b,
              w_gate, w_up, w_down, ln2_g, ln2_b):
    B, S, D = x.shape
    cos, sin = _rope_tables(S)
    c_act = jax.nn.silu(c)
    o1, o2, o3 = QKV_COLS, 2 * QKV_COLS, 3 * QKV_COLS
    for l in range(DEPTH):
        lam_init = 0.8 - 0.6 * math.exp(-0.3 * l)
        mod = c_act @ ada_w[l] + ada_b[l]
        shift1, scale1, gate1, shift2, scale2, gate2 = [m[:, None, :] for m in jnp.split(mod, 6, axis=-1)]

        h = x * (1.0 + scale1) + shift1
        proj = h @ w_in[l]
        lam = (jnp.exp(jnp.sum(lambda_q1[l] * lambda_k1[l]).astype(jnp.float32))
               - jnp.exp(jnp.sum(lambda_q2[l] * lambda_k2[l]).astype(jnp.float32)) + lam_init)
        attn_out = _diff_attention(proj[..., :o1], proj[..., o1:o2], proj[..., o2:o3],
                                   lam, lam_init, subln_g[l], cos, sin)
        sgu_out = _spatial_gating(jax.nn.gelu(proj[..., o3:], approximate=False),
                                  sgu_ln_g[l], sgu_ln_b[l], sgu_w[l], sgu_b[l])
        mix = jnp.concatenate([attn_out, sgu_out], axis=-1) @ w_o[l]
        x = _layernorm(ALPHA * x + gate1 * mix, ln1_g[l], ln1_b[l])

        h2 = x * (1.0 + scale2) + shift2
        ffn = _swiglu(h2, w_gate[l], w_up[l], w_down[l])
        x = _layernorm(ALPHA * x + gate2 * ffn, ln2_g[l], ln2_b[l])
    return x
```

```python
import math
from functools import partial

import jax
import jax.numpy as jnp
from jax import lax
from jax.experimental import pallas as pl
from jax.experimental.pallas import tpu as pltpu

D_MODEL = 1024
N_DIFF_HEADS = 4
DIFF_HEAD_DIM = 64
DIFF_VDIM = 2 * DIFF_HEAD_DIM
DIFF_WIDTH = N_DIFF_HEADS * DIFF_VDIM
SGU_WIDTH = D_MODEL - DIFF_WIDTH
SGU_GROUPS = 4
SGU_GROUP_DIM = SGU_WIDTH // SGU_GROUPS
CHUNK = 128
ROT_DIM = DIFF_HEAD_DIM // 4
ROPE_THETA = 500000.0
DEPTH = 1
ALPHA = (2 * DEPTH) ** 0.25
LN_EPS = 1e-5
QKV_COLS = DIFF_WIDTH

V7X_LANES = 128
V7X_VMEM_LIMIT_BYTES = 56 * 1024 * 1024

PROJ_ROWS = 512
FFN_ROWS = 512
ATTN_Q_ROWS = 256
ADA_COLS = 1536

_BF16 = jnp.bfloat16
_F32 = jnp.float32
_NT = (((1,), (1,)), ((), ()))


def _resident(shape):
    zeros = (0,) * len(shape)
    return pl.BlockSpec(shape, lambda *_: zeros, pipeline_mode=pl.Buffered(1))


def _layernorm_rows(y, g, b):
    mu = jnp.mean(y, axis=-1, keepdims=True)
    d = y - mu
    var = jnp.mean(d * d, axis=-1, keepdims=True)
    return d * lax.rsqrt(var + LN_EPS) * g + b


def _adaln_kernel(c_ref, w_ref, b_ref, o_ref):
    c = c_ref[...]
    c_act = (c * jax.nn.sigmoid(c)).astype(_BF16)
    acc = jnp.dot(c_act, w_ref[...].astype(_BF16), preferred_element_type=_F32)
    o_ref[...] = acc + b_ref[...]


def _adaln_mod(c, ada_w, ada_b):
    B, D = c.shape
    N = ada_w.shape[1]
    return pl.pallas_call(
        _adaln_kernel,
        grid=(N // ADA_COLS,),
        in_specs=[
            pl.BlockSpec((B, D), lambda j: (0, 0)),
            pl.BlockSpec((D, ADA_COLS), lambda j: (0, j)),
            pl.BlockSpec((1, ADA_COLS), lambda j: (0, j)),
        ],
        out_specs=pl.BlockSpec((B, ADA_COLS), lambda j: (0, j)),
        out_shape=jax.ShapeDtypeStruct((B, N), _F32),
        compiler_params=pltpu.CompilerParams(
            dimension_semantics=("arbitrary",), vmem_limit_bytes=V7X_VMEM_LIMIT_BYTES),
        name="adaln_mod",
    )(c, ada_w, ada_b)


def _rope(t, c_tab, s_lo, s_hi):
    up = pltpu.roll(t, V7X_LANES - ROT_DIM // 2, 1)
    dn = pltpu.roll(t, ROT_DIM // 2, 1)
    return t * c_tab + up * s_lo + dn * s_hi


def _in_proj_kernel(x_ref, scale_ref, shift_ref, w_ref, ropeq_ref, ropek_ref,
                    lng_ref, lnb_ref, sw_ref, sbt_ref,
                    q_ref, k_ref, v_ref, sgu_ref):
    rows = x_ref.shape[1]
    h = (x_ref[0] * (1.0 + scale_ref[0]) + shift_ref[0]).astype(_BF16)

    def proj(lo, width):
        return jnp.dot(h, w_ref[:, lo:lo + width], preferred_element_type=_F32)

    for out_ref, tabs, base in ((q_ref, ropeq_ref, 0), (k_ref, ropek_ref, QKV_COLS)):
        c_tab, s_lo, s_hi = tabs[0], tabs[1], tabs[2]
        for s in range(QKV_COLS // V7X_LANES):
            t = proj(base + s * V7X_LANES, V7X_LANES)
            out_ref[0, :, s * V7X_LANES:(s + 1) * V7X_LANES] = _rope(t, c_tab, s_lo, s_hi).astype(_BF16)

    v_ref[0] = proj(2 * QKV_COLS, QKV_COLS).astype(_BF16)

    o3 = 3 * QKV_COLS
    zu = proj(o3, SGU_WIDTH)
    zv = proj(o3 + SGU_WIDTH, SGU_WIDTH)
    inv_sqrt2 = 1.0 / math.sqrt(2.0)
    u = 0.5 * zu * (1.0 + lax.erf(zu * inv_sqrt2))
    vv = 0.5 * zv * (1.0 + lax.erf(zv * inv_sqrt2))
    vv = _layernorm_rows(vv, lng_ref[...], lnb_ref[...]).astype(_BF16)

    tri = (lax.broadcasted_iota(jnp.int32, (CHUNK, CHUNK), 0)
           >= lax.broadcasted_iota(jnp.int32, (CHUNK, CHUNK), 1))
    for g in range(SGU_GROUPS):
        w_g = jnp.where(tri, sw_ref[g], 0.0).astype(_BF16)
        b_g = sbt_ref[:, g:g + 1]
        lo = g * SGU_GROUP_DIM
        for n in range(rows // CHUNK):
            r0 = n * CHUNK
            sv = jnp.dot(w_g, vv[r0:r0 + CHUNK, lo:lo + SGU_GROUP_DIM],
                         preferred_element_type=_F32) + b_g
            sgu_ref[0, r0:r0 + CHUNK, lo:lo + SGU_GROUP_DIM] = (
                u[r0:r0 + CHUNK, lo:lo + SGU_GROUP_DIM] * sv).astype(_BF16)


def _in_proj(x, scale1, shift1, w_in, rope_q, rope_k, sgu_ln_g, sgu_ln_b, sgu_w, sgu_bt):
    B, S, D = x.shape
    P = w_in.shape[1]
    rows = PROJ_ROWS
    tok = lambda width: pl.BlockSpec((1, rows, width), lambda b, i: (b, i, 0))
    per_batch = pl.BlockSpec((1, 1, D), lambda b, i: (b, 0, 0))
    rope_spec = pl.BlockSpec((3, rows, V7X_LANES), lambda b, i: (0, i, 0))
    out_sd = jax.ShapeDtypeStruct((B, S, QKV_COLS), _BF16)
    return pl.pallas_call(
        _in_proj_kernel,
        grid=(B, S // rows),
        in_specs=[
            tok(D), per_batch, per_batch, _resident((D, P)), rope_spec, rope_spec,
            _resident((1, SGU_WIDTH)), _resident((1, SGU_WIDTH)),
            _resident((SGU_GROUPS, CHUNK, CHUNK)), _resident((CHUNK, SGU_GROUPS)),
        ],
        out_specs=[tok(QKV_COLS), tok(QKV_COLS), tok(QKV_COLS), tok(SGU_WIDTH)],
        out_shape=[out_sd, out_sd, out_sd, jax.ShapeDtypeStruct((B, S, SGU_WIDTH), _BF16)],
        compiler_params=pltpu.CompilerParams(
            dimension_semantics=("arbitrary", "arbitrary"),
            vmem_limit_bytes=V7X_VMEM_LIMIT_BYTES),
        name="in_proj",
    )(x, scale1, shift1, w_in, rope_q, rope_k, sgu_ln_g, sgu_ln_b, sgu_w, sgu_bt)


def _diff_attn_kernel(lq1_ref, lk1_ref, lq2_ref, lk2_ref, g_ref, q_ref, k_ref, v_ref, o_ref,
                      *, lam_init):
    S = q_ref.shape[1]
    tq = ATTN_Q_ROWS
    lam = (jnp.exp(jnp.sum(lq1_ref[...] * lk1_ref[...], axis=-1, keepdims=True))
           - jnp.exp(jnp.sum(lq2_ref[...] * lk2_ref[...], axis=-1, keepdims=True))
           + lam_init)
    first = lax.broadcasted_iota(jnp.int32, (1, V7X_LANES), 1) < DIFF_HEAD_DIM
    row = lax.broadcasted_iota(jnp.int32, (2 * tq, tq), 0)
    col = lax.broadcasted_iota(jnp.int32, (2 * tq, tq), 1)
    causal = col <= jnp.where(row >= tq, row - tq, row)
    neg = jnp.finfo(_F32).min
    gain = g_ref[...] * (1.0 - lam_init)

    for i in range(S // tq):
        r0 = i * tq
        q = q_ref[0, r0:r0 + tq, :]
        zero = jnp.zeros_like(q)
        qs = jnp.concatenate([jnp.where(first, q, zero), jnp.where(first, zero, q)], axis=0)
        s_d = lax.dot_general(qs, k_ref[0, r0:r0 + tq, :], _NT, preferred_element_type=_F32)
        s_d = jnp.where(causal, s_d, neg)
        m = jnp.max(s_d, axis=-1, keepdims=True)
        if i > 0:
            s_o = lax.dot_general(qs, k_ref[0, 0:r0, :], _NT, preferred_element_type=_F32)
            m = jnp.maximum(m, jnp.max(s_o, axis=-1, keepdims=True))
        p_d = jnp.exp(s_d - m)
        l = jnp.sum(p_d, axis=-1, keepdims=True)
        if i > 0:
            p_o = jnp.exp(s_o - m)
            l = l + jnp.sum(p_o, axis=-1, keepdims=True)
        r = 1.0 / l
        r1 = r[:tq]
        r2 = lam * r[tq:]
        a_d = (p_d[:tq] * r1 - p_d[tq:] * r2).astype(_BF16)
        o = jnp.dot(a_d, v_ref[0, r0:r0 + tq, :], preferred_element_type=_F32)
        if i > 0:
            a_o = (p_o[:tq] * r1 - p_o[tq:] * r2).astype(_BF16)
            o = o + jnp.dot(a_o, v_ref[0, 0:r0, :], preferred_element_type=_F32)
        y = o * lax.rsqrt(jnp.mean(o * o, axis=-1, keepdims=True) + LN_EPS) * gain
        o_ref[0, r0:r0 + tq, :] = y.astype(_BF16)


def _diff_attn(q, k, v, lq1, lk1, lq2, lk2, subln_g, lam_init):
    B, S, _ = q.shape
    head = pl.BlockSpec((1, S, DIFF_VDIM), lambda b, h: (b, 0, h))
    vec = _resident((1, DIFF_HEAD_DIM))
    return pl.pallas_call(
        partial(_diff_attn_kernel, lam_init=lam_init),
        grid=(B, N_DIFF_HEADS),
        in_specs=[vec, vec, vec, vec, _resident((1, DIFF_VDIM)), head, head, head],
        out_specs=head,
        out_shape=jax.ShapeDtypeStruct((B, S, DIFF_WIDTH), _BF16),
        compiler_params=pltpu.CompilerParams(
            dimension_semantics=("arbitrary", "arbitrary"),
            vmem_limit_bytes=V7X_VMEM_LIMIT_BYTES),
        name="diff_attn",
    )(lq1, lk1, lq2, lk2, subln_g, q, k, v)


def _out_ffn_kernel(x_ref, attn_ref, sgu_ref, gate1_ref, shift2_ref, scale2_ref, gate2_ref,
                    wo_ref, ln1g_ref, ln1b_ref, wg_ref, wu_ref, wd_ref, ln2g_ref, ln2b_ref,
                    o_ref):
    mix = (jnp.dot(attn_ref[0], wo_ref[0:DIFF_WIDTH, :], preferred_element_type=_F32)
           + jnp.dot(sgu_ref[0], wo_ref[DIFF_WIDTH:, :], preferred_element_type=_F32))
    x1 = _layernorm_rows(ALPHA * x_ref[0] + gate1_ref[0] * mix, ln1g_ref[...], ln1b_ref[...])
    h2 = (x1 * (1.0 + scale2_ref[0]) + shift2_ref[0]).astype(_BF16)
    g = jnp.dot(h2, wg_ref[...], preferred_element_type=_F32)
    u = jnp.dot(h2, wu_ref[...], preferred_element_type=_F32)
    a = (g * jax.nn.sigmoid(g) * u).astype(_BF16)
    ffn = jnp.dot(a, wd_ref[...], preferred_element_type=_F32)
    o_ref[0] = _layernorm_rows(ALPHA * x1 + gate2_ref[0] * ffn, ln2g_ref[...], ln2b_ref[...])


def _out_ffn(x, attn, sgu, gate1, shift2, scale2, gate2, w_o, ln1_g, ln1_b,
             w_gate, w_up, w_down, ln2_g, ln2_b):
    B, S, D = x.shape
    F = w_gate.shape[1]
    rows = FFN_ROWS
    tok = lambda width: pl.BlockSpec((1, rows, width), lambda b, i: (b, i, 0))
    per_batch = pl.BlockSpec((1, 1, D), lambda b, i: (b, 0, 0))
    vec = _resident((1, D))
    return pl.pallas_call(
        _out_ffn_kernel,
        grid=(B, S // rows),
        in_specs=[
            tok(D), tok(DIFF_WIDTH), tok(SGU_WIDTH), per_batch, per_batch, per_batch, per_batch,
            _resident((D, D)), vec, vec, _resident((D, F)), _resident((D, F)), _resident((F, D)),
            vec, vec,
        ],
        out_specs=tok(D),
        out_shape=jax.ShapeDtypeStruct((B, S, D), x.dtype),
        compiler_params=pltpu.CompilerParams(
            dimension_semantics=("arbitrary", "arbitrary"),
            vmem_limit_bytes=V7X_VMEM_LIMIT_BYTES),
        name="out_ffn",
    )(x, attn, sgu, gate1, shift2, scale2, gate2, w_o, ln1_g, ln1_b,
      w_gate, w_up, w_down, ln2_g, ln2_b)


def _rope_tables(seq, scale):
    half = ROT_DIM // 2
    inv_freq = ROPE_THETA ** (-jnp.arange(half, dtype=_F32) * 2.0 / ROT_DIM)
    ang = jnp.arange(seq, dtype=_F32)[:, None] * inv_freq[None, :]
    cos, sin = jnp.cos(ang), jnp.sin(ang)
    pad = jnp.zeros((seq, DIFF_HEAD_DIM - ROT_DIM), _F32)
    zer = jnp.zeros((seq, half), _F32)
    c_tab = jnp.concatenate([cos, cos, pad + 1.0], axis=-1)
    s_lo = jnp.concatenate([-sin, zer, pad], axis=-1)
    s_hi = jnp.concatenate([zer, sin, pad], axis=-1)
    tabs = jnp.stack([c_tab, s_lo, s_hi]) * scale
    return jnp.tile(tabs, (1, 1, V7X_LANES // DIFF_HEAD_DIM))


def kernel(x, c, ada_w, ada_b, w_in, lambda_q1, lambda_k1, lambda_q2, lambda_k2, subln_g,
           sgu_ln_g, sgu_ln_b, sgu_w, sgu_b, w_o, ln1_g, ln1_b, w_gate, w_up, w_down,
           ln2_g, ln2_b):
    B, S, D = x.shape
    rope_q = _rope_tables(S, DIFF_HEAD_DIM ** -0.5)
    rope_k = _rope_tables(S, 1.0)
    for l in range(DEPTH):
        lam_init = 0.8 - 0.6 * math.exp(-0.3 * l)
        mod = _adaln_mod(c, ada_w[l], ada_b[l][None, :])
        shift1, scale1, gate1, shift2, scale2, gate2 = [
            m[:, None, :] for m in jnp.split(mod, 6, axis=-1)]
        q, k, v, sgu = _in_proj(
            x, scale1, shift1, w_in[l].astype(_BF16), rope_q, rope_k,
            sgu_ln_g[l][None, :], sgu_ln_b[l][None, :], sgu_w[l], sgu_b[l].T)
        attn = _diff_attn(q, k, v, lambda_q1[l][None, :], lambda_k1[l][None, :],
                          lambda_q2[l][None, :], lambda_k2[l][None, :],
                          subln_g[l][None, :], lam_init)
        x = _out_ffn(x, attn, sgu, gate1, shift2, scale2, gate2,
                     w_o[l].astype(_BF16), ln1_g[l][None, :], ln1_b[l][None, :],
                     w_gate[l].astype(_BF16), w_up[l].astype(_BF16), w_down[l].astype(_BF16),
                     ln2_g[l][None, :], ln2_b[l][None, :])
    return x
```

```python
import math
from functools import partial

import jax
import jax.numpy as jnp
from jax import lax
from jax.experimental import pallas as pl
from jax.experimental.pallas import tpu as pltpu

D_MODEL = 1024
N_DIFF_HEADS = 4
DIFF_HEAD_DIM = 64
DIFF_VDIM = 2 * DIFF_HEAD_DIM
DIFF_WIDTH = N_DIFF_HEADS * DIFF_VDIM
SGU_WIDTH = D_MODEL - DIFF_WIDTH
SGU_GROUPS = 4
SGU_GROUP_DIM = SGU_WIDTH // SGU_GROUPS
CHUNK = 128
ROT_DIM = DIFF_HEAD_DIM // 4
ROPE_THETA = 500000.0
DEPTH = 1
ALPHA = (2 * DEPTH) ** 0.25
LN_EPS = 1e-5
QKV_COLS = DIFF_WIDTH

V7X_LANES = 128
V7X_VMEM_LIMIT_BYTES = 56 * 1024 * 1024

PROJ_ROWS = 512
FFN_ROWS = 512
ATTN_Q_ROWS = 256
ADA_COLS = 1536

_BF16 = jnp.bfloat16
_F32 = jnp.float32
_NT = (((1,), (1,)), ((), ()))


def _resident(shape):
    zeros = (0,) * len(shape)
    return pl.BlockSpec(shape, lambda *_: zeros, pipeline_mode=pl.Buffered(1))


def _layernorm_rows(y, g, b):
    mu = jnp.mean(y, axis=-1, keepdims=True)
    d = y - mu
    var = jnp.mean(d * d, axis=-1, keepdims=True)
    return d * lax.rsqrt(var + LN_EPS) * g + b


def _adaln_kernel(c_ref, w_ref, b_ref, o_ref):
    c = c_ref[...]
    c_act = (c * jax.nn.sigmoid(c)).astype(_BF16)
    acc = jnp.dot(c_act, w_ref[...].astype(_BF16), preferred_element_type=_F32)
    o_ref[...] = acc + b_ref[...]


def _adaln_mod(c, ada_w, ada_b):
    B, D = c.shape
    N = ada_w.shape[1]
    return pl.pallas_call(
        _adaln_kernel,
        grid=(N // ADA_COLS,),
        in_specs=[
            pl.BlockSpec((B, D), lambda j: (0, 0)),
            pl.BlockSpec((D, ADA_COLS), lambda j: (0, j)),
            pl.BlockSpec((1, ADA_COLS), lambda j: (0, j)),
        ],
        out_specs=pl.BlockSpec((B, ADA_COLS), lambda j: (0, j)),
        out_shape=jax.ShapeDtypeStruct((B, N), _F32),
        compiler_params=pltpu.CompilerParams(
            dimension_semantics=("arbitrary",), vmem_limit_bytes=V7X_VMEM_LIMIT_BYTES),
        name="adaln_mod",
    )(c, ada_w, ada_b)


def _rope(t, c_tab, s_lo, s_hi):
    up = pltpu.roll(t, V7X_LANES - ROT_DIM // 2, 1)
    dn = pltpu.roll(t, ROT_DIM // 2, 1)
    return t * c_tab + up * s_lo + dn * s_hi


def _in_proj_kernel(x_ref, scale_ref, shift_ref, w_ref, ropeq_ref, ropek_ref,
                    lng_ref, lnb_ref, sw_ref, sbt_ref,
                    q_ref, k_ref, v_ref, sgu_ref):
    rows = x_ref.shape[1]
    h = (x_ref[0] * (1.0 + scale_ref[0]) + shift_ref[0]).astype(_BF16)

    def proj(lo, width):
        return jnp.dot(h, w_ref[:, lo:lo + width], preferred_element_type=_F32)

    for out_ref, tabs, base in ((q_ref, ropeq_ref, 0), (k_ref, ropek_ref, QKV_COLS)):
        c_tab, s_lo, s_hi = tabs[0], tabs[1], tabs[2]
        for s in range(QKV_COLS // V7X_LANES):
            t = proj(base + s * V7X_LANES, V7X_LANES)
            out_ref[0, :, s * V7X_LANES:(s + 1) * V7X_LANES] = _rope(t, c_tab, s_lo, s_hi).astype(_BF16)

    v_ref[0] = proj(2 * QKV_COLS, QKV_COLS).astype(_BF16)

    o3 = 3 * QKV_COLS
    zu = proj(o3, SGU_WIDTH)
    zv = proj(o3 + SGU_WIDTH, SGU_WIDTH)
    inv_sqrt2 = 1.0 / math.sqrt(2.0)
    u = 0.5 * zu * (1.0 + lax.erf(zu * inv_sqrt2))
    vv = 0.5 * zv * (1.0 + lax.erf(zv * inv_sqrt2))
    vv = _layernorm_rows(vv, lng_ref[...], lnb_ref[...]).astype(_BF16)

    tri = (lax.broadcasted_iota(jnp.int32, (CHUNK, CHUNK), 0)
           >= lax.broadcasted_iota(jnp.int32, (CHUNK, CHUNK), 1))
    for g in range(SGU_GROUPS):
        w_g = jnp.where(tri, sw_ref[g], 0.0).astype(_BF16)
        b_g = sbt_ref[:, g:g + 1]
        lo = g * SGU_GROUP_DIM
        for n in range(rows // CHUNK):
            r0 = n * CHUNK
            sv = jnp.dot(w_g, vv[r0:r0 + CHUNK, lo:lo + SGU_GROUP_DIM],
                         preferred_element_type=_F32) + b_g
            sgu_ref[0, r0:r0 + CHUNK, lo:lo + SGU_GROUP_DIM] = (
                u[r0:r0 + CHUNK, lo:lo + SGU_GROUP_DIM] * sv).astype(_BF16)


def _in_proj(x, scale1, shift1, w_in, rope_q, rope_k, sgu_ln_g, sgu_ln_b, sgu_w, sgu_bt):
    B, S, D = x.shape
    P = w_in.shape[1]
    rows = PROJ_ROWS
    tok = lambda width: pl.BlockSpec((1, rows, width), lambda b, i: (b, i, 0))
    per_batch = pl.BlockSpec((1, 1, D), lambda b, i: (b, 0, 0))
    rope_spec = pl.BlockSpec((3, rows, V7X_LANES), lambda b, i: (0, i, 0))
    out_sd = jax.ShapeDtypeStruct((B, S, QKV_COLS), _BF16)
    return pl.pallas_call(
        _in_proj_kernel,
        grid=(B, S // rows),
        in_specs=[
            tok(D), per_batch, per_batch, _resident((D, P)), rope_spec, rope_spec,
            _resident((1, SGU_WIDTH)), _resident((1, SGU_WIDTH)),
            _resident((SGU_GROUPS, CHUNK, CHUNK)), _resident((CHUNK, SGU_GROUPS)),
        ],
        out_specs=[tok(QKV_COLS), tok(QKV_COLS), tok(QKV_COLS), tok(SGU_WIDTH)],
        out_shape=[out_sd, out_sd, out_sd, jax.ShapeDtypeStruct((B, S, SGU_WIDTH), _BF16)],
        compiler_params=pltpu.CompilerParams(
            dimension_semantics=("arbitrary", "arbitrary"),
            vmem_limit_bytes=V7X_VMEM_LIMIT_BYTES),
        name="in_proj",
    )(x, scale1, shift1, w_in, rope_q, rope_k, sgu_ln_g, sgu_ln_b, sgu_w, sgu_bt)


def _diff_attn_kernel(lq1_ref, lk1_ref, lq2_ref, lk2_ref, g_ref, q_ref, k_ref, v_ref, o_ref,
                      vaug_ref, *, lam_init):
    S = q_ref.shape[1]
    tq = ATTN_Q_ROWS
    lam = (jnp.exp(jnp.sum(lq1_ref[...] * lk1_ref[...], axis=-1, keepdims=True))
           - jnp.exp(jnp.sum(lq2_ref[...] * lk2_ref[...], axis=-1, keepdims=True))
           + lam_init)
    lane = lax.broadcasted_iota(jnp.int32, (1, V7X_LANES), 1)
    first = lane < DIFF_HEAD_DIM
    causal = (lax.broadcasted_iota(jnp.int32, (tq, tq), 1)
              <= lax.broadcasted_iota(jnp.int32, (tq, tq), 0))
    neg = jnp.finfo(_F32).min
    gain = g_ref[...] * (1.0 - lam_init)

    vaug_ref[:, 0:DIFF_VDIM] = v_ref[0]
    vaug_ref[:, DIFF_VDIM:] = jnp.broadcast_to(jnp.where(lane == 0, 1.0, 0.0), (S, V7X_LANES)).astype(_BF16)

    def scores(i):
        n = (i + 1) * tq
        q = q_ref[0, i * tq:n, :]
        zero = jnp.zeros_like(q)
        return [lax.dot_general(qc, k_ref[0, 0:n, :], _NT, preferred_element_type=_F32)
                for qc in (jnp.where(first, q, zero), jnp.where(first, zero, q))]

    def weighted_values(s, i):
        r0, n = i * tq, (i + 1) * tq
        s_d = jnp.where(causal, s[:, r0:n], neg)
        m = jnp.max(s_d, axis=-1, keepdims=True)
        if i > 0:
            m = jnp.maximum(m, jnp.max(s[:, 0:r0], axis=-1, keepdims=True))
            p = jnp.concatenate([jnp.exp2(s[:, 0:r0] - m), jnp.exp2(s_d - m)], axis=1)
        else:
            p = jnp.exp2(s_d - m)
        return jnp.dot(p.astype(_BF16), vaug_ref[0:n, :], preferred_element_type=_F32)

    n_blocks = S // tq
    s_next = scores(0)
    for i in range(n_blocks):
        s_cur = s_next
        if i + 1 < n_blocks:
            s_next = scores(i + 1)
        acc1 = weighted_values(s_cur[0], i)
        acc2 = weighted_values(s_cur[1], i)
        r1 = 1.0 / acc1[:, DIFF_VDIM:DIFF_VDIM + 1]
        r2 = lam / acc2[:, DIFF_VDIM:DIFF_VDIM + 1]
        o = acc1[:, 0:DIFF_VDIM] * r1 - acc2[:, 0:DIFF_VDIM] * r2
        y = o * lax.rsqrt(jnp.mean(o * o, axis=-1, keepdims=True) + LN_EPS) * gain
        o_ref[0, i * tq:(i + 1) * tq, :] = y.astype(_BF16)


def _diff_attn(q, k, v, lq1, lk1, lq2, lk2, subln_g, lam_init):
    B, S, _ = q.shape
    head = pl.BlockSpec((1, S, DIFF_VDIM), lambda b, h: (b, 0, h))
    vec = _resident((1, DIFF_HEAD_DIM))
    return pl.pallas_call(
        partial(_diff_attn_kernel, lam_init=lam_init),
        grid=(B, N_DIFF_HEADS),
        in_specs=[vec, vec, vec, vec, _resident((1, DIFF_VDIM)), head, head, head],
        out_specs=head,
        out_shape=jax.ShapeDtypeStruct((B, S, DIFF_WIDTH), _BF16),
        scratch_shapes=[pltpu.VMEM((S, DIFF_VDIM + V7X_LANES), _BF16)],
        compiler_params=pltpu.CompilerParams(
            dimension_semantics=("arbitrary", "arbitrary"),
            vmem_limit_bytes=V7X_VMEM_LIMIT_BYTES),
        name="diff_attn",
    )(lq1, lk1, lq2, lk2, subln_g, q, k, v)


def _out_ffn_kernel(x_ref, attn_ref, sgu_ref, gate1_ref, shift2_ref, scale2_ref, gate2_ref,
                    wo_ref, ln1g_ref, ln1b_ref, wg_ref, wu_ref, wd_ref, ln2g_ref, ln2b_ref,
                    o_ref):
    mix = (jnp.dot(attn_ref[0], wo_ref[0:DIFF_WIDTH, :], preferred_element_type=_F32)
           + jnp.dot(sgu_ref[0], wo_ref[DIFF_WIDTH:, :], preferred_element_type=_F32))
    x1 = _layernorm_rows(ALPHA * x_ref[0] + gate1_ref[0] * mix, ln1g_ref[...], ln1b_ref[...])
    h2 = (x1 * (1.0 + scale2_ref[0]) + shift2_ref[0]).astype(_BF16)
    g = jnp.dot(h2, wg_ref[...], preferred_element_type=_F32)
    u = jnp.dot(h2, wu_ref[...], preferred_element_type=_F32)
    a = (g * jax.nn.sigmoid(g) * u).astype(_BF16)
    ffn = jnp.dot(a, wd_ref[...], preferred_element_type=_F32)
    o_ref[0] = _layernorm_rows(ALPHA * x1 + gate2_ref[0] * ffn, ln2g_ref[...], ln2b_ref[...])


def _out_ffn(x, attn, sgu, gate1, shift2, scale2, gate2, w_o, ln1_g, ln1_b,
             w_gate, w_up, w_down, ln2_g, ln2_b):
    B, S, D = x.shape
    F = w_gate.shape[1]
    rows = FFN_ROWS
    tok = lambda width: pl.BlockSpec((1, rows, width), lambda b, i: (b, i, 0))
    per_batch = pl.BlockSpec((1, 1, D), lambda b, i: (b, 0, 0))
    vec = _resident((1, D))
    return pl.pallas_call(
        _out_ffn_kernel,
        grid=(B, S // rows),
        in_specs=[
            tok(D), tok(DIFF_WIDTH), tok(SGU_WIDTH), per_batch, per_batch, per_batch, per_batch,
            _resident((D, D)), vec, vec, _resident((D, F)), _resident((D, F)), _resident((F, D)),
            vec, vec,
        ],
        out_specs=tok(D),
        out_shape=jax.ShapeDtypeStruct((B, S, D), x.dtype),
        compiler_params=pltpu.CompilerParams(
            dimension_semantics=("arbitrary", "arbitrary"),
            vmem_limit_bytes=V7X_VMEM_LIMIT_BYTES),
        name="out_ffn",
    )(x, attn, sgu, gate1, shift2, scale2, gate2, w_o, ln1_g, ln1_b,
      w_gate, w_up, w_down, ln2_g, ln2_b)


def _rope_tables(seq, scale):
    half = ROT_DIM // 2
    inv_freq = ROPE_THETA ** (-jnp.arange(half, dtype=_F32) * 2.0 / ROT_DIM)
    ang = jnp.arange(seq, dtype=_F32)[:, None] * inv_freq[None, :]
    cos, sin = jnp.cos(ang), jnp.sin(ang)
    pad = jnp.zeros((seq, DIFF_HEAD_DIM - ROT_DIM), _F32)
    zer = jnp.zeros((seq, half), _F32)
    c_tab = jnp.concatenate([cos, cos, pad + 1.0], axis=-1)
    s_lo = jnp.concatenate([-sin, zer, pad], axis=-1)
    s_hi = jnp.concatenate([zer, sin, pad], axis=-1)
    tabs = jnp.stack([c_tab, s_lo, s_hi]) * scale
    return jnp.tile(tabs, (1, 1, V7X_LANES // DIFF_HEAD_DIM))


def kernel(x, c, ada_w, ada_b, w_in, lambda_q1, lambda_k1, lambda_q2, lambda_k2, subln_g,
           sgu_ln_g, sgu_ln_b, sgu_w, sgu_b, w_o, ln1_g, ln1_b, w_gate, w_up, w_down,
           ln2_g, ln2_b):
    B, S, D = x.shape
    rope_q = _rope_tables(S, DIFF_HEAD_DIM ** -0.5 * math.log2(math.e))
    rope_k = _rope_tables(S, 1.0)
    for l in range(DEPTH):
        lam_init = 0.8 - 0.6 * math.exp(-0.3 * l)
        mod = _adaln_mod(c, ada_w[l], ada_b[l][None, :])
        shift1, scale1, gate1, shift2, scale2, gate2 = [
            m[:, None, :] for m in jnp.split(mod, 6, axis=-1)]
        q, k, v, sgu = _in_proj(
            x, scale1, shift1, w_in[l].astype(_BF16), rope_q, rope_k,
            sgu_ln_g[l][None, :], sgu_ln_b[l][None, :], sgu_w[l], sgu_b[l].T)
        attn = _diff_attn(q, k, v, lambda_q1[l][None, :], lambda_k1[l][None, :],
                          lambda_q2[l][None, :], lambda_k2[l][None, :],
                          subln_g[l][None, :], lam_init)
        x = _out_ffn(x, attn, sgu, gate1, shift2, scale2, gate2,
                     w_o[l].astype(_BF16), ln1_g[l][None, :], ln1_b[l][None, :],
                     w_gate[l].astype(_BF16), w_up[l].astype(_BF16), w_down[l].astype(_BF16),
                     ln2_g[l][None, :], ln2_b[l][None, :])
    return x
```

```python
import math
from functools import partial

import jax
import jax.numpy as jnp
from jax import lax
from jax.experimental import pallas as pl
from jax.experimental.pallas import tpu as pltpu

D_MODEL = 1024
N_DIFF_HEADS = 4
DIFF_HEAD_DIM = 64
DIFF_VDIM = 2 * DIFF_HEAD_DIM
DIFF_WIDTH = N_DIFF_HEADS * DIFF_VDIM
SGU_WIDTH = D_MODEL - DIFF_WIDTH
SGU_GROUPS = 4
SGU_GROUP_DIM = SGU_WIDTH // SGU_GROUPS
CHUNK = 128
ROT_DIM = DIFF_HEAD_DIM // 4
ROPE_THETA = 500000.0
DEPTH = 1
ALPHA = (2 * DEPTH) ** 0.25
LN_EPS = 1e-5
QKV_COLS = DIFF_WIDTH

V7X_LANES = 128
V7X_VMEM_LIMIT_BYTES = 56 * 1024 * 1024

PROJ_ROWS = 1024
FFN_ROWS = 1024
FFN_SLAB_ROWS = 256
ATTN_Q_ROWS = 256
ADA_COLS = 1536

_BF16 = jnp.bfloat16
_F32 = jnp.float32
_NT = (((1,), (1,)), ((), ()))


def _resident(shape):
    zeros = (0,) * len(shape)
    return pl.BlockSpec(shape, lambda *_: zeros, pipeline_mode=pl.Buffered(1))


def _layernorm_rows(y, g, b):
    mu = jnp.mean(y, axis=-1, keepdims=True)
    d = y - mu
    var = jnp.mean(d * d, axis=-1, keepdims=True)
    return d * lax.rsqrt(var + LN_EPS) * g + b


def _adaln_kernel(c_ref, w_ref, b_ref, o_ref):
    c = c_ref[...]
    c_act = (c * jax.nn.sigmoid(c)).astype(_BF16)
    acc = jnp.dot(c_act, w_ref[...].astype(_BF16), preferred_element_type=_F32)
    o_ref[...] = acc + b_ref[...]


def _adaln_mod(c, ada_w, ada_b):
    B, D = c.shape
    N = ada_w.shape[1]
    return pl.pallas_call(
        _adaln_kernel,
        grid=(N // ADA_COLS,),
        in_specs=[
            pl.BlockSpec((B, D), lambda j: (0, 0)),
            pl.BlockSpec((D, ADA_COLS), lambda j: (0, j)),
            pl.BlockSpec((1, ADA_COLS), lambda j: (0, j)),
        ],
        out_specs=pl.BlockSpec((B, ADA_COLS), lambda j: (0, j)),
        out_shape=jax.ShapeDtypeStruct((B, N), _F32),
        compiler_params=pltpu.CompilerParams(
            dimension_semantics=("arbitrary",), vmem_limit_bytes=V7X_VMEM_LIMIT_BYTES),
        name="adaln_mod",
    )(c, ada_w, ada_b)


def _rope(t, c_tab, s_lo, s_hi):
    up = pltpu.roll(t, V7X_LANES - ROT_DIM // 2, 1)
    dn = pltpu.roll(t, ROT_DIM // 2, 1)
    return t * c_tab + up * s_lo + dn * s_hi


def _in_proj_kernel(x_ref, scale_ref, shift_ref, w_ref, ropeq_ref, ropek_ref,
                    lng_ref, lnb_ref, sw_ref, sbt_ref,
                    q_ref, k_ref, v_ref, sgu_ref):
    rows = x_ref.shape[1]
    n_chunks = rows // CHUNK
    h = (x_ref[0] * (1.0 + scale_ref[0]) + shift_ref[0]).astype(_BF16)

    def proj(lo, width):
        return jnp.dot(h, w_ref[:, lo:lo + width], preferred_element_type=_F32)

    o3 = 3 * QKV_COLS
    zu = proj(o3, SGU_WIDTH)
    zv = proj(o3 + SGU_WIDTH, SGU_WIDTH)
    inv_sqrt2 = 1.0 / math.sqrt(2.0)
    u = 0.5 * zu * (1.0 + lax.erf(zu * inv_sqrt2))
    vv = 0.5 * zv * (1.0 + lax.erf(zv * inv_sqrt2))
    vv = _layernorm_rows(vv, lng_ref[...], lnb_ref[...]).astype(_BF16)

    for out_ref, tabs, base in ((q_ref, ropeq_ref, 0), (k_ref, ropek_ref, QKV_COLS)):
        c_tab, s_lo, s_hi = tabs[0], tabs[1], tabs[2]
        t = proj(base, QKV_COLS)
        for s in range(QKV_COLS // V7X_LANES):
            sl = slice(s * V7X_LANES, (s + 1) * V7X_LANES)
            out_ref[0, :, sl] = _rope(t[:, sl], c_tab, s_lo, s_hi).astype(_BF16)

    v_ref[0] = proj(2 * QKV_COLS, QKV_COLS).astype(_BF16)

    tri = (lax.broadcasted_iota(jnp.int32, (CHUNK, CHUNK), 0)
           >= lax.broadcasted_iota(jnp.int32, (CHUNK, CHUNK), 1))
    for g in range(SGU_GROUPS):
        cols = slice(g * SGU_GROUP_DIM, (g + 1) * SGU_GROUP_DIM)
        w_g = jnp.where(tri, sw_ref[g], 0.0).astype(_BF16)
        b_g = sbt_ref[:, g:g + 1]
        v_g = jnp.concatenate([vv[n * CHUNK:(n + 1) * CHUNK, cols] for n in range(n_chunks)], axis=1)
        sv = jnp.dot(w_g, v_g, preferred_element_type=_F32) + b_g
        for n in range(n_chunks):
            rws = slice(n * CHUNK, (n + 1) * CHUNK)
            sgu_ref[0, rws, cols] = (
                u[rws, cols] * sv[:, n * SGU_GROUP_DIM:(n + 1) * SGU_GROUP_DIM]).astype(_BF16)


def _in_proj(x, scale1, shift1, w_in, rope_q, rope_k, sgu_ln_g, sgu_ln_b, sgu_w, sgu_bt):
    B, S, D = x.shape
    P = w_in.shape[1]
    rows = PROJ_ROWS
    tok = lambda width: pl.BlockSpec((1, rows, width), lambda b, i: (b, i, 0))
    per_batch = pl.BlockSpec((1, 1, D), lambda b, i: (b, 0, 0))
    rope_spec = pl.BlockSpec((3, rows, V7X_LANES), lambda b, i: (0, i, 0))
    out_sd = jax.ShapeDtypeStruct((B, S, QKV_COLS), _BF16)
    return pl.pallas_call(
        _in_proj_kernel,
        grid=(B, S // rows),
        in_specs=[
            tok(D), per_batch, per_batch, _resident((D, P)), rope_spec, rope_spec,
            _resident((1, SGU_WIDTH)), _resident((1, SGU_WIDTH)),
            _resident((SGU_GROUPS, CHUNK, CHUNK)), _resident((CHUNK, SGU_GROUPS)),
        ],
        out_specs=[tok(QKV_COLS), tok(QKV_COLS), tok(QKV_COLS), tok(SGU_WIDTH)],
        out_shape=[out_sd, out_sd, out_sd, jax.ShapeDtypeStruct((B, S, SGU_WIDTH), _BF16)],
        compiler_params=pltpu.CompilerParams(
            dimension_semantics=("arbitrary", "arbitrary"),
            vmem_limit_bytes=V7X_VMEM_LIMIT_BYTES),
        name="in_proj",
    )(x, scale1, shift1, w_in, rope_q, rope_k, sgu_ln_g, sgu_ln_b, sgu_w, sgu_bt)


def _diff_attn_kernel(lq1_ref, lk1_ref, lq2_ref, lk2_ref, g_ref, q_ref, k_ref, v_ref, o_ref,
                      vaug_ref, *, lam_init):
    S = q_ref.shape[1]
    tq = ATTN_Q_ROWS
    lam = (jnp.exp(jnp.sum(lq1_ref[...] * lk1_ref[...], axis=-1, keepdims=True))
           - jnp.exp(jnp.sum(lq2_ref[...] * lk2_ref[...], axis=-1, keepdims=True))
           + lam_init)
    lane = lax.broadcasted_iota(jnp.int32, (1, V7X_LANES), 1)
    first = lane < DIFF_HEAD_DIM
    causal = (lax.broadcasted_iota(jnp.int32, (tq, tq), 1)
              <= lax.broadcasted_iota(jnp.int32, (tq, tq), 0))
    neg = jnp.finfo(_F32).min
    gain = g_ref[...] * (1.0 - lam_init)

    vaug_ref[:, 0:DIFF_VDIM] = v_ref[0]
    vaug_ref[:, DIFF_VDIM:] = jnp.broadcast_to(jnp.where(lane == 0, 1.0, 0.0), (S, V7X_LANES)).astype(_BF16)

    def scores(i):
        n = (i + 1) * tq
        q = q_ref[0, i * tq:n, :]
        zero = jnp.zeros_like(q)
        return [lax.dot_general(qc, k_ref[0, 0:n, :], _NT, preferred_element_type=_F32)
                for qc in (jnp.where(first, q, zero), jnp.where(first, zero, q))]

    def weighted_values(s, i):
        r0, n = i * tq, (i + 1) * tq
        s_d = jnp.where(causal, s[:, r0:n], neg)
        m = jnp.max(s_d, axis=-1, keepdims=True)
        if i > 0:
            m = jnp.maximum(m, jnp.max(s[:, 0:r0], axis=-1, keepdims=True))
            p = jnp.concatenate([jnp.exp2(s[:, 0:r0] - m), jnp.exp2(s_d - m)], axis=1)
        else:
            p = jnp.exp2(s_d - m)
        return jnp.dot(p.astype(_BF16), vaug_ref[0:n, :], preferred_element_type=_F32)

    n_blocks = S // tq
    s_next = scores(0)
    for i in range(n_blocks):
        s_cur = s_next
        if i + 1 < n_blocks:
            s_next = scores(i + 1)
        acc1 = weighted_values(s_cur[0], i)
        acc2 = weighted_values(s_cur[1], i)
        r1 = 1.0 / acc1[:, DIFF_VDIM:DIFF_VDIM + 1]
        r2 = lam / acc2[:, DIFF_VDIM:DIFF_VDIM + 1]
        o = acc1[:, 0:DIFF_VDIM] * r1 - acc2[:, 0:DIFF_VDIM] * r2
        y = o * lax.rsqrt(jnp.mean(o * o, axis=-1, keepdims=True) + LN_EPS) * gain
        o_ref[0, i * tq:(i + 1) * tq, :] = y.astype(_BF16)


def _diff_attn(q, k, v, lq1, lk1, lq2, lk2, subln_g, lam_init):
    B, S, _ = q.shape
    head = pl.BlockSpec((1, S, DIFF_VDIM), lambda b, h: (b, 0, h))
    vec = _resident((1, DIFF_HEAD_DIM))
    return pl.pallas_call(
        partial(_diff_attn_kernel, lam_init=lam_init),
        grid=(B, N_DIFF_HEADS),
        in_specs=[vec, vec, vec, vec, _resident((1, DIFF_VDIM)), head, head, head],
        out_specs=head,
        out_shape=jax.ShapeDtypeStruct((B, S, DIFF_WIDTH), _BF16),
        scratch_shapes=[pltpu.VMEM((S, DIFF_VDIM + V7X_LANES), _BF16)],
        compiler_params=pltpu.CompilerParams(
            dimension_semantics=("arbitrary", "arbitrary"),
            vmem_limit_bytes=V7X_VMEM_LIMIT_BYTES),
        name="diff_attn",
    )(lq1, lk1, lq2, lk2, subln_g, q, k, v)


def _out_ffn_kernel(x_ref, attn_ref, sgu_ref, gate1_ref, shift2_ref, scale2_ref, gate2_ref,
                    wo_ref, ln1g_ref, ln1b_ref, wg_ref, wu_ref, wd_ref, ln2g_ref, ln2b_ref,
                    o_ref):
    rows = x_ref.shape[1]
    slabs = [slice(r, r + FFN_SLAB_ROWS) for r in range(0, rows, FFN_SLAB_ROWS)]
    mix = [jnp.dot(attn_ref[0, rs, :], wo_ref[0:DIFF_WIDTH, :], preferred_element_type=_F32)
           + jnp.dot(sgu_ref[0, rs, :], wo_ref[DIFF_WIDTH:, :], preferred_element_type=_F32)
           for rs in slabs]
    for rs, mix_s in zip(slabs, mix):
        x1 = _layernorm_rows(ALPHA * x_ref[0, rs, :] + gate1_ref[0] * mix_s,
                             ln1g_ref[...], ln1b_ref[...])
        h2 = (x1 * (1.0 + scale2_ref[0]) + shift2_ref[0]).astype(_BF16)
        g = jnp.dot(h2, wg_ref[...], preferred_element_type=_F32)
        u = jnp.dot(h2, wu_ref[...], preferred_element_type=_F32)
        a = (g * jax.nn.sigmoid(g) * u).astype(_BF16)
        ffn = jnp.dot(a, wd_ref[...], preferred_element_type=_F32)
        o_ref[0, rs, :] = _layernorm_rows(ALPHA * x1 + gate2_ref[0] * ffn,
                                          ln2g_ref[...], ln2b_ref[...])


def _out_ffn(x, attn, sgu, gate1, shift2, scale2, gate2, w_o, ln1_g, ln1_b,
             w_gate, w_up, w_down, ln2_g, ln2_b):
    B, S, D = x.shape
    F = w_gate.shape[1]
    rows = FFN_ROWS
    tok = lambda width: pl.BlockSpec((1, rows, width), lambda b, i: (b, i, 0))
    per_batch = pl.BlockSpec((1, 1, D), lambda b, i: (b, 0, 0))
    vec = _resident((1, D))
    return pl.pallas_call(
        _out_ffn_kernel,
        grid=(B, S // rows),
        in_specs=[
            tok(D), tok(DIFF_WIDTH), tok(SGU_WIDTH), per_batch, per_batch, per_batch, per_batch,
            _resident((D, D)), vec, vec, _resident((D, F)), _resident((D, F)), _resident((F, D)),
            vec, vec,
        ],
        out_specs=tok(D),
        out_shape=jax.ShapeDtypeStruct((B, S, D), x.dtype),
        compiler_params=pltpu.CompilerParams(
            dimension_semantics=("arbitrary", "arbitrary"),
            vmem_limit_bytes=V7X_VMEM_LIMIT_BYTES),
        name="out_ffn",
    )(x, attn, sgu, gate1, shift2, scale2, gate2, w_o, ln1_g, ln1_b,
      w_gate, w_up, w_down, ln2_g, ln2_b)


def _rope_tables(seq, scale):
    half = ROT_DIM // 2
    inv_freq = ROPE_THETA ** (-jnp.arange(half, dtype=_F32) * 2.0 / ROT_DIM)
    ang = jnp.arange(seq, dtype=_F32)[:, None] * inv_freq[None, :]
    cos, sin = jnp.cos(ang), jnp.sin(ang)
    pad = jnp.zeros((seq, DIFF_HEAD_DIM - ROT_DIM), _F32)
    zer = jnp.zeros((seq, half), _F32)
    c_tab = jnp.concatenate([cos, cos, pad + 1.0], axis=-1)
    s_lo = jnp.concatenate([-sin, zer, pad], axis=-1)
    s_hi = jnp.concatenate([zer, sin, pad], axis=-1)
    tabs = jnp.stack([c_tab, s_lo, s_hi]) * scale
    return jnp.tile(tabs, (1, 1, V7X_LANES // DIFF_HEAD_DIM))


def kernel(x, c, ada_w, ada_b, w_in, lambda_q1, lambda_k1, lambda_q2, lambda_k2, subln_g,
           sgu_ln_g, sgu_ln_b, sgu_w, sgu_b, w_o, ln1_g, ln1_b, w_gate, w_up, w_down,
           ln2_g, ln2_b):
    B, S, D = x.shape
    rope_q = _rope_tables(S, DIFF_HEAD_DIM ** -0.5 * math.log2(math.e))
    rope_k = _rope_tables(S, 1.0)
    for l in range(DEPTH):
        lam_init = 0.8 - 0.6 * math.exp(-0.3 * l)
        mod = _adaln_mod(c, ada_w[l], ada_b[l][None, :])
        shift1, scale1, gate1, shift2, scale2, gate2 = [
            m[:, None, :] for m in jnp.split(mod, 6, axis=-1)]
        q, k, v, sgu = _in_proj(
            x, scale1, shift1, w_in[l].astype(_BF16), rope_q, rope_k,
            sgu_ln_g[l][None, :], sgu_ln_b[l][None, :], sgu_w[l], sgu_b[l].T)
        attn = _diff_attn(q, k, v, lambda_q1[l][None, :], lambda_k1[l][None, :],
                          lambda_q2[l][None, :], lambda_k2[l][None, :],
                          subln_g[l][None, :], lam_init)
        x = _out_ffn(x, attn, sgu, gate1, shift2, scale2, gate2,
                     w_o[l].astype(_BF16), ln1_g[l][None, :], ln1_b[l][None, :],
                     w_gate[l].astype(_BF16), w_up[l].astype(_BF16), w_down[l].astype(_BF16),
                     ln2_g[l][None, :], ln2_b[l][None, :])
    return x
```

```python
import math
from functools import partial

import jax
import jax.numpy as jnp
from jax import lax
from jax.experimental import pallas as pl
from jax.experimental.pallas import tpu as pltpu

D_MODEL = 1024
N_DIFF_HEADS = 4
DIFF_HEAD_DIM = 64
DIFF_VDIM = 2 * DIFF_HEAD_DIM
DIFF_WIDTH = N_DIFF_HEADS * DIFF_VDIM
SGU_WIDTH = D_MODEL - DIFF_WIDTH
SGU_GROUPS = 4
SGU_GROUP_DIM = SGU_WIDTH // SGU_GROUPS
CHUNK = 128
ROT_DIM = DIFF_HEAD_DIM // 4
ROPE_THETA = 500000.0
DEPTH = 1
ALPHA = (2 * DEPTH) ** 0.25
LN_EPS = 1e-5
QKV_COLS = DIFF_WIDTH

V7X_LANES = 128
V7X_VMEM_LIMIT_BYTES = 56 * 1024 * 1024

PROJ_ROWS = 1024
FFN_ROWS = 1024
FFN_SLAB_ROWS = 256
ATTN_Q_ROWS = 256
ATTN_ROWSUM_LIMIT = 2.0 ** 60
ADA_COLS = 1536

_BF16 = jnp.bfloat16
_F32 = jnp.float32
_NT = (((1,), (1,)), ((), ()))


def _resident(shape):
    zeros = (0,) * len(shape)
    return pl.BlockSpec(shape, lambda *_: zeros, pipeline_mode=pl.Buffered(1))


def _layernorm_rows(y, g, b):
    mu = jnp.mean(y, axis=-1, keepdims=True)
    d = y - mu
    var = jnp.mean(d * d, axis=-1, keepdims=True)
    return d * lax.rsqrt(var + LN_EPS) * g + b


def _adaln_kernel(c_ref, w_ref, b_ref, o_ref):
    c = c_ref[...]
    c_act = (c * jax.nn.sigmoid(c)).astype(_BF16)
    acc = jnp.dot(c_act, w_ref[...].astype(_BF16), preferred_element_type=_F32)
    o_ref[...] = acc + b_ref[...]


def _adaln_mod(c, ada_w, ada_b):
    B, D = c.shape
    N = ada_w.shape[1]
    return pl.pallas_call(
        _adaln_kernel,
        grid=(N // ADA_COLS,),
        in_specs=[
            pl.BlockSpec((B, D), lambda j: (0, 0)),
            pl.BlockSpec((D, ADA_COLS), lambda j: (0, j)),
            pl.BlockSpec((1, ADA_COLS), lambda j: (0, j)),
        ],
        out_specs=pl.BlockSpec((B, ADA_COLS), lambda j: (0, j)),
        out_shape=jax.ShapeDtypeStruct((B, N), _F32),
        compiler_params=pltpu.CompilerParams(
            dimension_semantics=("arbitrary",), vmem_limit_bytes=V7X_VMEM_LIMIT_BYTES),
        name="adaln_mod",
    )(c, ada_w, ada_b)


def _rope(t, c_tab, s_lo, s_hi):
    up = pltpu.roll(t, V7X_LANES - ROT_DIM // 2, 1)
    dn = pltpu.roll(t, ROT_DIM // 2, 1)
    return t * c_tab + up * s_lo + dn * s_hi


def _in_proj_kernel(x_ref, scale_ref, shift_ref, w_ref, ropeq_ref, ropek_ref,
                    lng_ref, lnb_ref, sw_ref, sbt_ref,
                    q_ref, k_ref, v_ref, sgu_ref):
    rows = x_ref.shape[1]
    n_chunks = rows // CHUNK
    h = (x_ref[0] * (1.0 + scale_ref[0]) + shift_ref[0]).astype(_BF16)

    def proj(lo, width):
        return jnp.dot(h, w_ref[:, lo:lo + width], preferred_element_type=_F32)

    o3 = 3 * QKV_COLS
    zu = proj(o3, SGU_WIDTH)
    zv = proj(o3 + SGU_WIDTH, SGU_WIDTH)
    inv_sqrt2 = 1.0 / math.sqrt(2.0)
    u = 0.5 * zu * (1.0 + lax.erf(zu * inv_sqrt2))
    vv = 0.5 * zv * (1.0 + lax.erf(zv * inv_sqrt2))
    vv = _layernorm_rows(vv, lng_ref[...], lnb_ref[...]).astype(_BF16)

    for out_ref, tabs, base in ((q_ref, ropeq_ref, 0), (k_ref, ropek_ref, QKV_COLS)):
        c_tab, s_lo, s_hi = tabs[0], tabs[1], tabs[2]
        t = proj(base, QKV_COLS)
        for s in range(QKV_COLS // V7X_LANES):
            sl = slice(s * V7X_LANES, (s + 1) * V7X_LANES)
            out_ref[0, :, sl] = _rope(t[:, sl], c_tab, s_lo, s_hi).astype(_BF16)

    v_ref[0] = proj(2 * QKV_COLS, QKV_COLS).astype(_BF16)

    tri = (lax.broadcasted_iota(jnp.int32, (CHUNK, CHUNK), 0)
           >= lax.broadcasted_iota(jnp.int32, (CHUNK, CHUNK), 1))
    for g in range(SGU_GROUPS):
        cols = slice(g * SGU_GROUP_DIM, (g + 1) * SGU_GROUP_DIM)
        w_g = jnp.where(tri, sw_ref[g], 0.0).astype(_BF16)
        b_g = sbt_ref[:, g:g + 1]
        v_g = jnp.concatenate([vv[n * CHUNK:(n + 1) * CHUNK, cols] for n in range(n_chunks)], axis=1)
        sv = jnp.dot(w_g, v_g, preferred_element_type=_F32) + b_g
        for n in range(n_chunks):
            rws = slice(n * CHUNK, (n + 1) * CHUNK)
            sgu_ref[0, rws, cols] = (
                u[rws, cols] * sv[:, n * SGU_GROUP_DIM:(n + 1) * SGU_GROUP_DIM]).astype(_BF16)


def _in_proj(x, scale1, shift1, w_in, rope_q, rope_k, sgu_ln_g, sgu_ln_b, sgu_w, sgu_bt):
    B, S, D = x.shape
    P = w_in.shape[1]
    rows = PROJ_ROWS
    tok = lambda width: pl.BlockSpec((1, rows, width), lambda b, i: (b, i, 0))
    per_batch = pl.BlockSpec((1, 1, D), lambda b, i: (b, 0, 0))
    rope_spec = pl.BlockSpec((3, rows, V7X_LANES), lambda b, i: (0, i, 0))
    out_sd = jax.ShapeDtypeStruct((B, S, QKV_COLS), _BF16)
    return pl.pallas_call(
        _in_proj_kernel,
        grid=(B, S // rows),
        in_specs=[
            tok(D), per_batch, per_batch, _resident((D, P)), rope_spec, rope_spec,
            _resident((1, SGU_WIDTH)), _resident((1, SGU_WIDTH)),
            _resident((SGU_GROUPS, CHUNK, CHUNK)), _resident((CHUNK, SGU_GROUPS)),
        ],
        out_specs=[tok(QKV_COLS), tok(QKV_COLS), tok(QKV_COLS), tok(SGU_WIDTH)],
        out_shape=[out_sd, out_sd, out_sd, jax.ShapeDtypeStruct((B, S, SGU_WIDTH), _BF16)],
        compiler_params=pltpu.CompilerParams(
            dimension_semantics=("arbitrary", "arbitrary"),
            vmem_limit_bytes=V7X_VMEM_LIMIT_BYTES),
        name="in_proj",
    )(x, scale1, shift1, w_in, rope_q, rope_k, sgu_ln_g, sgu_ln_b, sgu_w, sgu_bt)


def _attn_setup(lq1_ref, lk1_ref, lq2_ref, lk2_ref, g_ref, lam_init):
    tq = ATTN_Q_ROWS
    lam = (jnp.exp(jnp.sum(lq1_ref[...] * lk1_ref[...], axis=-1, keepdims=True))
           - jnp.exp(jnp.sum(lq2_ref[...] * lk2_ref[...], axis=-1, keepdims=True))
           + lam_init)
    lane = lax.broadcasted_iota(jnp.int32, (1, V7X_LANES), 1)
    causal = (lax.broadcasted_iota(jnp.int32, (tq, tq), 1)
              <= lax.broadcasted_iota(jnp.int32, (tq, tq), 0))
    gain = g_ref[...] * (1.0 - lam_init)
    return lam, lane, lane < DIFF_HEAD_DIM, causal, gain


def _subln_store(o_ref, i, o, gain):
    tq = ATTN_Q_ROWS
    y = o * lax.rsqrt(jnp.mean(o * o, axis=-1, keepdims=True) + LN_EPS) * gain
    o_ref[0, i * tq:(i + 1) * tq, :] = y.astype(_BF16)


def _diff_attn_fast_kernel(lq1_ref, lk1_ref, lq2_ref, lk2_ref, g_ref, q_ref, k_ref, v_ref,
                           o_ref, lmax_ref, k1_ref, k2_ref, vpad_ref, *, lam_init):
    S = q_ref.shape[1]
    tq = ATTN_Q_ROWS
    lam, lane, first, causal, gain = _attn_setup(lq1_ref, lk1_ref, lq2_ref, lk2_ref, g_ref, lam_init)
    neg = jnp.finfo(_F32).min
    in1 = jnp.where(first, 1.0, 0.0)
    free1 = jnp.where(lane == DIFF_HEAD_DIM, 1.0, 0.0)
    free2 = jnp.where(lane == 0, 1.0, 0.0)

    k = k_ref[0].astype(_F32)
    k1_ref[...] = (k * in1 + free1).astype(_BF16)
    k2_ref[...] = (k * (1.0 - in1) + free2).astype(_BF16)
    vpad_ref[:, 0:DIFF_VDIM] = v_ref[0]
    vpad_ref[:, DIFF_VDIM:] = jnp.zeros((S, V7X_LANES), _BF16)

    def scores(i):
        n = (i + 1) * tq
        q = q_ref[0, i * tq:n, :]
        prod = q.astype(_F32) * k_ref[0, i * tq:n, :].astype(_F32)
        d1 = jnp.sum(jnp.where(first, prod, 0.0), axis=-1, keepdims=True)
        d2 = jnp.sum(jnp.where(first, 0.0, prod), axis=-1, keepdims=True)
        q1 = jnp.where(first, q, (-d1 * free1).astype(_BF16))
        q2 = jnp.where(first, (-d2 * free2).astype(_BF16), q)
        return [lax.dot_general(q1, k1_ref[0:n, :], _NT, preferred_element_type=_F32),
                lax.dot_general(q2, k2_ref[0:n, :], _NT, preferred_element_type=_F32)]

    def exp_scores(s, i):
        r0, n = i * tq, (i + 1) * tq
        p = jnp.exp2(jnp.where(causal, s[:, r0:n], neg))
        if i > 0:
            p = jnp.concatenate([jnp.exp2(s[:, 0:r0]), p], axis=1)
        return p, jnp.sum(p, axis=-1, keepdims=True)

    n_blocks = S // tq
    lmax = jnp.zeros((tq, 1), _F32)
    s_next = scores(0)
    for i in range(n_blocks):
        s_cur = s_next
        if i + 1 < n_blocks:
            s_next = scores(i + 1)
        n = (i + 1) * tq
        p1, l1 = exp_scores(s_cur[0], i)
        p2, l2 = exp_scores(s_cur[1], i)
        lmax = jnp.maximum(lmax, jnp.maximum(l1, l2))
        a = (p1 - (lam * l1 / l2) * p2).astype(_BF16)
        kh = n // 2
        o = (jnp.dot(a[:, 0:kh], vpad_ref[0:kh, :], preferred_element_type=_F32)
             + jnp.dot(a[:, kh:n], vpad_ref[kh:n, :], preferred_element_type=_F32))
        _subln_store(o_ref, i, o[:, 0:DIFF_VDIM] * (1.0 / l1), gain)
    lmax_ref[...] = jnp.broadcast_to(jnp.max(lmax, axis=0, keepdims=True), lmax_ref.shape)


def _diff_attn_exact_kernel(lq1_ref, lk1_ref, lq2_ref, lk2_ref, g_ref, q_ref, k_ref, v_ref, o_ref,
                            vaug_ref, *, lam_init):
    S = q_ref.shape[1]
    tq = ATTN_Q_ROWS
    lam, lane, first, causal, gain = _attn_setup(lq1_ref, lk1_ref, lq2_ref, lk2_ref, g_ref, lam_init)
    neg = jnp.finfo(_F32).min

    vaug_ref[:, 0:DIFF_VDIM] = v_ref[0]
    vaug_ref[:, DIFF_VDIM:] = jnp.broadcast_to(jnp.where(lane == 0, 1.0, 0.0), (S, V7X_LANES)).astype(_BF16)

    def scores(i):
        n = (i + 1) * tq
        q = q_ref[0, i * tq:n, :]
        zero = jnp.zeros_like(q)
        return [lax.dot_general(qc, k_ref[0, 0:n, :], _NT, preferred_element_type=_F32)
                for qc in (jnp.where(first, q, zero), jnp.where(first, zero, q))]

    def weighted_values(s, i):
        r0, n = i * tq, (i + 1) * tq
        s_d = jnp.where(causal, s[:, r0:n], neg)
        m = jnp.max(s_d, axis=-1, keepdims=True)
        if i > 0:
            m = jnp.maximum(m, jnp.max(s[:, 0:r0], axis=-1, keepdims=True))
            p = jnp.concatenate([jnp.exp2(s[:, 0:r0] - m), jnp.exp2(s_d - m)], axis=1)
        else:
            p = jnp.exp2(s_d - m)
        return jnp.dot(p.astype(_BF16), vaug_ref[0:n, :], preferred_element_type=_F32)

    n_blocks = S // tq
    s_next = scores(0)
    for i in range(n_blocks):
        s_cur = s_next
        if i + 1 < n_blocks:
            s_next = scores(i + 1)
        acc1 = weighted_values(s_cur[0], i)
        acc2 = weighted_values(s_cur[1], i)
        r1 = 1.0 / acc1[:, DIFF_VDIM:DIFF_VDIM + 1]
        r2 = lam / acc2[:, DIFF_VDIM:DIFF_VDIM + 1]
        _subln_store(o_ref, i, acc1[:, 0:DIFF_VDIM] * r1 - acc2[:, 0:DIFF_VDIM] * r2, gain)


def _diff_attn(q, k, v, lq1, lk1, lq2, lk2, subln_g, lam_init):
    B, S, _ = q.shape
    head = pl.BlockSpec((1, S, DIFF_VDIM), lambda b, h: (b, 0, h))
    vec = _resident((1, DIFF_HEAD_DIM))
    in_specs = [vec, vec, vec, vec, _resident((1, DIFF_VDIM)), head, head, head]
    params = pltpu.CompilerParams(dimension_semantics=("arbitrary", "arbitrary"),
                                  vmem_limit_bytes=V7X_VMEM_LIMIT_BYTES)
    out_sd = jax.ShapeDtypeStruct((B, S, DIFF_WIDTH), _BF16)
    args = (lq1, lk1, lq2, lk2, subln_g, q, k, v)
    wide = DIFF_VDIM + V7X_LANES

    fast, lmax = pl.pallas_call(
        partial(_diff_attn_fast_kernel, lam_init=lam_init),
        grid=(B, N_DIFF_HEADS),
        in_specs=in_specs,
        out_specs=[head, pl.BlockSpec((1, 1, 8, V7X_LANES), lambda b, h: (b, h, 0, 0))],
        out_shape=[out_sd, jax.ShapeDtypeStruct((B, N_DIFF_HEADS, 8, V7X_LANES), _F32)],
        scratch_shapes=[pltpu.VMEM((S, DIFF_VDIM), _BF16), pltpu.VMEM((S, DIFF_VDIM), _BF16),
                        pltpu.VMEM((S, wide), _BF16)],
        compiler_params=params,
        name="diff_attn_fast",
    )(*args)

    def exact():
        return pl.pallas_call(
            partial(_diff_attn_exact_kernel, lam_init=lam_init),
            grid=(B, N_DIFF_HEADS),
            in_specs=in_specs,
            out_specs=head,
            out_shape=out_sd,
            scratch_shapes=[pltpu.VMEM((S, wide), _BF16)],
            compiler_params=params,
            name="diff_attn_exact",
        )(*args)

    in_range = jnp.max(lmax) <= ATTN_ROWSUM_LIMIT
    return lax.cond(in_range, lambda: fast, exact)


def _out_ffn_kernel(x_ref, attn_ref, sgu_ref, gate1_ref, shift2_ref, scale2_ref, gate2_ref,
                    wo_ref, ln1g_ref, ln1b_ref, wg_ref, wu_ref, wd_ref, ln2g_ref, ln2b_ref,
                    o_ref):
    rows = x_ref.shape[1]
    slabs = [slice(r, r + FFN_SLAB_ROWS) for r in range(0, rows, FFN_SLAB_ROWS)]
    mix = [jnp.dot(attn_ref[0, rs, :], wo_ref[0:DIFF_WIDTH, :], preferred_element_type=_F32)
           + jnp.dot(sgu_ref[0, rs, :], wo_ref[DIFF_WIDTH:, :], preferred_element_type=_F32)
           for rs in slabs]
    for rs, mix_s in zip(slabs, mix):
        x1 = _layernorm_rows(ALPHA * x_ref[0, rs, :] + gate1_ref[0] * mix_s,
                             ln1g_ref[...], ln1b_ref[...])
        h2 = (x1 * (1.0 + scale2_ref[0]) + shift2_ref[0]).astype(_BF16)
        g = jnp.dot(h2, wg_ref[...], preferred_element_type=_F32)
        u = jnp.dot(h2, wu_ref[...], preferred_element_type=_F32)
        a = (g * jax.nn.sigmoid(g) * u).astype(_BF16)
        ffn = jnp.dot(a, wd_ref[...], preferred_element_type=_F32)
        o_ref[0, rs, :] = _layernorm_rows(ALPHA * x1 + gate2_ref[0] * ffn,
                                          ln2g_ref[...], ln2b_ref[...])


def _out_ffn(x, attn, sgu, gate1, shift2, scale2, gate2, w_o, ln1_g, ln1_b,
             w_gate, w_up, w_down, ln2_g, ln2_b):
    B, S, D = x.shape
    F = w_gate.shape[1]
    rows = FFN_ROWS
    tok = lambda width: pl.BlockSpec((1, rows, width), lambda b, i: (b, i, 0))
    per_batch = pl.BlockSpec((1, 1, D), lambda b, i: (b, 0, 0))
    vec = _resident((1, D))
    return pl.pallas_call(
        _out_ffn_kernel,
        grid=(B, S // rows),
        in_specs=[
            tok(D), tok(DIFF_WIDTH), tok(SGU_WIDTH), per_batch, per_batch, per_batch, per_batch,
            _resident((D, D)), vec, vec, _resident((D, F)), _resident((D, F)), _resident((F, D)),
            vec, vec,
        ],
        out_specs=tok(D),
        out_shape=jax.ShapeDtypeStruct((B, S, D), x.dtype),
        compiler_params=pltpu.CompilerParams(
            dimension_semantics=("arbitrary", "arbitrary"),
            vmem_limit_bytes=V7X_VMEM_LIMIT_BYTES),
        name="out_ffn",
    )(x, attn, sgu, gate1, shift2, scale2, gate2, w_o, ln1_g, ln1_b,
      w_gate, w_up, w_down, ln2_g, ln2_b)


def _rope_tables(seq, scale):
    half = ROT_DIM // 2
    inv_freq = ROPE_THETA ** (-jnp.arange(half, dtype=_F32) * 2.0 / ROT_DIM)
    ang = jnp.arange(seq, dtype=_F32)[:, None] * inv_freq[None, :]
    cos, sin = jnp.cos(ang), jnp.sin(ang)
    pad = jnp.zeros((seq, DIFF_HEAD_DIM - ROT_DIM), _F32)
    zer = jnp.zeros((seq, half), _F32)
    c_tab = jnp.concatenate([cos, cos, pad + 1.0], axis=-1)
    s_lo = jnp.concatenate([-sin, zer, pad], axis=-1)
    s_hi = jnp.concatenate([zer, sin, pad], axis=-1)
    tabs = jnp.stack([c_tab, s_lo, s_hi]) * scale
    return jnp.tile(tabs, (1, 1, V7X_LANES // DIFF_HEAD_DIM))


def kernel(x, c, ada_w, ada_b, w_in, lambda_q1, lambda_k1, lambda_q2, lambda_k2, subln_g,
           sgu_ln_g, sgu_ln_b, sgu_w, sgu_b, w_o, ln1_g, ln1_b, w_gate, w_up, w_down,
           ln2_g, ln2_b):
    B, S, D = x.shape
    rope_q = _rope_tables(S, DIFF_HEAD_DIM ** -0.5 * math.log2(math.e))
    rope_k = _rope_tables(S, 1.0)
    for l in range(DEPTH):
        lam_init = 0.8 - 0.6 * math.exp(-0.3 * l)
        mod = _adaln_mod(c, ada_w[l], ada_b[l][None, :])
        shift1, scale1, gate1, shift2, scale2, gate2 = [
            m[:, None, :] for m in jnp.split(mod, 6, axis=-1)]
        q, k, v, sgu = _in_proj(
            x, scale1, shift1, w_in[l].astype(_BF16), rope_q, rope_k,
            sgu_ln_g[l][None, :], sgu_ln_b[l][None, :], sgu_w[l], sgu_b[l].T)
        attn = _diff_attn(q, k, v, lambda_q1[l][None, :], lambda_k1[l][None, :],
                          lambda_q2[l][None, :], lambda_k2[l][None, :],
                          subln_g[l][None, :], lam_init)
        x = _out_ffn(x, attn, sgu, gate1, shift2, scale2, gate2,
                     w_o[l].astype(_BF16), ln1_g[l][None, :], ln1_b[l][None, :],
                     w_gate[l].astype(_BF16), w_up[l].astype(_BF16), w_down[l].astype(_BF16),
                     ln2_g[l][None, :], ln2_b[l][None, :])
    return x
```

```python
import math
from functools import partial

import numpy as np
import jax
import jax.numpy as jnp
from jax import lax
from jax.experimental import pallas as pl
from jax.experimental.pallas import tpu as pltpu

D_MODEL = 1024
N_DIFF_HEADS = 4
DIFF_HEAD_DIM = 64
DIFF_VDIM = 2 * DIFF_HEAD_DIM
DIFF_WIDTH = N_DIFF_HEADS * DIFF_VDIM
SGU_WIDTH = D_MODEL - DIFF_WIDTH
SGU_GROUPS = 4
SGU_GROUP_DIM = SGU_WIDTH // SGU_GROUPS
CHUNK = 128
ROT_DIM = DIFF_HEAD_DIM // 4
ROPE_THETA = 500000.0
DEPTH = 1
ALPHA = (2 * DEPTH) ** 0.25
LN_EPS = 1e-5
QKV_COLS = DIFF_WIDTH
MOD_SHIFT1, MOD_SCALE1, MOD_GATE1, MOD_SHIFT2, MOD_SCALE2, MOD_GATE2 = range(6)

V7X_LANES = 128
V7X_BF16_SUBLANES = 16
V7X_VMEM_LIMIT_BYTES = 56 * 1024 * 1024

PROJ_ROWS = 1024
FFN_ROWS = 1024
FFN_SLAB_ROWS = 256
ATTN_Q_ROWS = 256
ATTN_ROWSUM_LIMIT = 2.0 ** 60
ADA_COLS = 768

_BF16 = jnp.bfloat16
_F32 = jnp.float32
_NT = (((1,), (1,)), ((), ()))


def _resident(shape):
    zeros = (0,) * len(shape)
    return pl.BlockSpec(shape, lambda *_: zeros, pipeline_mode=pl.Buffered(1))


def _layernorm_rows(y, g, b):
    mu = jnp.mean(y, axis=-1, keepdims=True)
    d = y - mu
    var = jnp.mean(d * d, axis=-1, keepdims=True)
    return d * lax.rsqrt(var + LN_EPS) * g + b


def _adaln_kernel(c_ref, w_ref, b_ref, win_ref, o_ref, win16_ref):
    c = c_ref[...]
    c_act = (c * jax.nn.sigmoid(c)).astype(_BF16)
    acc = jnp.dot(c_act, w_ref[...].astype(_BF16), preferred_element_type=_F32)
    o_ref[...] = acc + b_ref[...]
    win16_ref[...] = win_ref[...].astype(_BF16)


def _adaln_mod(c, ada_w, ada_b, w_in):
    B, D = c.shape
    N = ada_w.shape[1]
    steps = N // ADA_COLS
    win_spec = pl.BlockSpec((w_in.shape[0] // steps, w_in.shape[1]), lambda j: (j, 0))
    return pl.pallas_call(
        _adaln_kernel,
        grid=(steps,),
        in_specs=[
            pl.BlockSpec((B, D), lambda j: (0, 0)),
            pl.BlockSpec((D, ADA_COLS), lambda j: (0, j)),
            pl.BlockSpec((1, ADA_COLS), lambda j: (0, j)),
            win_spec,
        ],
        out_specs=[pl.BlockSpec((B, ADA_COLS), lambda j: (0, j)), win_spec],
        out_shape=[jax.ShapeDtypeStruct((B, N), _F32), jax.ShapeDtypeStruct(w_in.shape, _BF16)],
        compiler_params=pltpu.CompilerParams(
            dimension_semantics=("arbitrary",), vmem_limit_bytes=V7X_VMEM_LIMIT_BYTES),
        name="adaln_mod",
    )(c, ada_w, ada_b, w_in)


def _rope(t, c_tab, s_lo, s_hi):
    up = pltpu.roll(t, V7X_LANES - ROT_DIM // 2, 1)
    dn = pltpu.roll(t, ROT_DIM // 2, 1)
    return t * c_tab + up * s_lo + dn * s_hi


def _in_proj_kernel(x_ref, scale_ref, shift_ref, w_ref, ropeq_ref, ropek_ref,
                    lng_ref, lnb_ref, sw_ref, sbt_ref, *rest):
    n_cast = (len(rest) - 4) // 2
    cast_in, (q_ref, k_ref, v_ref, sgu_ref), cast_out = (
        rest[:n_cast], rest[n_cast:n_cast + 4], rest[n_cast + 4:])
    for src, dst in zip(cast_in, cast_out):
        dst[...] = src[...].astype(_BF16)

    rows = x_ref.shape[1]
    n_chunks = rows // CHUNK
    b = pl.program_id(0)
    scale, shift = scale_ref[pl.ds(b, 1), :], shift_ref[pl.ds(b, 1), :]
    h = (x_ref[0] * (1.0 + scale) + shift).astype(_BF16)

    def proj(lo, width):
        return jnp.dot(h, w_ref[:, lo:lo + width], preferred_element_type=_F32)

    o3 = 3 * QKV_COLS
    zu = proj(o3, SGU_WIDTH)
    zv = proj(o3 + SGU_WIDTH, SGU_WIDTH)
    inv_sqrt2 = 1.0 / math.sqrt(2.0)
    u = 0.5 * zu * (1.0 + lax.erf(zu * inv_sqrt2))
    vv = 0.5 * zv * (1.0 + lax.erf(zv * inv_sqrt2))
    vv = _layernorm_rows(vv, lng_ref[...], lnb_ref[...]).astype(_BF16)

    for out_ref, tabs, base in ((q_ref, ropeq_ref, 0), (k_ref, ropek_ref, QKV_COLS)):
        c_tab, s_lo, s_hi = tabs[0], tabs[1], tabs[2]
        t = proj(base, QKV_COLS)
        for s in range(QKV_COLS // V7X_LANES):
            sl = slice(s * V7X_LANES, (s + 1) * V7X_LANES)
            out_ref[0, :, sl] = _rope(t[:, sl], c_tab, s_lo, s_hi).astype(_BF16)

    v_ref[0] = proj(2 * QKV_COLS, QKV_COLS).astype(_BF16)

    tri = (lax.broadcasted_iota(jnp.int32, (CHUNK, CHUNK), 0)
           >= lax.broadcasted_iota(jnp.int32, (CHUNK, CHUNK), 1))
    for g in range(SGU_GROUPS):
        cols = slice(g * SGU_GROUP_DIM, (g + 1) * SGU_GROUP_DIM)
        w_g = jnp.where(tri, sw_ref[g], 0.0).astype(_BF16)
        b_g = sbt_ref[:, g:g + 1]
        v_g = jnp.concatenate([vv[n * CHUNK:(n + 1) * CHUNK, cols] for n in range(n_chunks)], axis=1)
        sv = jnp.dot(w_g, v_g, preferred_element_type=_F32) + b_g
        for n in range(n_chunks):
            rws = slice(n * CHUNK, (n + 1) * CHUNK)
            sgu_ref[0, rws, cols] = (
                u[rws, cols] * sv[:, n * SGU_GROUP_DIM:(n + 1) * SGU_GROUP_DIM]).astype(_BF16)


def _row_block_spec(n_rows, n_cols, n_steps, steps_per_batch):
    n_blocks = n_steps
    while n_rows % (n_blocks * V7X_BF16_SUBLANES):
        n_blocks //= 2
    repeat = n_steps // n_blocks
    return pl.BlockSpec((n_rows // n_blocks, n_cols),
                        lambda b, i: ((b * steps_per_batch + i) // repeat, 0))


def _mod_spec(B, D, which):
    return pl.BlockSpec((B, D), lambda b, i: (0, which))


def _in_proj(x, mod, w_in, rope_q, rope_k, sgu_ln_g, sgu_ln_b, sgu_w, sgu_bt, later_weights):
    B, S, D = x.shape
    P = w_in.shape[1]
    rows = PROJ_ROWS
    steps_per_batch = S // rows
    tok = lambda width: pl.BlockSpec((1, rows, width), lambda b, i: (b, i, 0))
    rope_spec = pl.BlockSpec((3, rows, V7X_LANES), lambda b, i: (0, i, 0))
    out_sd = jax.ShapeDtypeStruct((B, S, QKV_COLS), _BF16)
    cast_specs = [_row_block_spec(w.shape[0], w.shape[1], B * steps_per_batch, steps_per_batch)
                  for w in later_weights]
    outs = pl.pallas_call(
        _in_proj_kernel,
        grid=(B, steps_per_batch),
        in_specs=[
            tok(D), _mod_spec(B, D, MOD_SCALE1), _mod_spec(B, D, MOD_SHIFT1),
            _resident((D, P)), rope_spec, rope_spec,
            _resident((1, SGU_WIDTH)), _resident((1, SGU_WIDTH)),
            _resident((SGU_GROUPS, CHUNK, CHUNK)), _resident((CHUNK, SGU_GROUPS)),
        ] + cast_specs,
        out_specs=[tok(QKV_COLS), tok(QKV_COLS), tok(QKV_COLS), tok(SGU_WIDTH)] + cast_specs,
        out_shape=[out_sd, out_sd, out_sd, jax.ShapeDtypeStruct((B, S, SGU_WIDTH), _BF16)]
        + [jax.ShapeDtypeStruct(w.shape, _BF16) for w in later_weights],
        compiler_params=pltpu.CompilerParams(
            dimension_semantics=("arbitrary", "arbitrary"),
            vmem_limit_bytes=V7X_VMEM_LIMIT_BYTES),
        name="in_proj",
    )(x, mod, mod, w_in, rope_q, rope_k, sgu_ln_g, sgu_ln_b, sgu_w, sgu_bt, *later_weights)
    return outs[:4], outs[4:]


def _attn_setup(lq1_ref, lk1_ref, lq2_ref, lk2_ref, g_ref, lam_init):
    tq = ATTN_Q_ROWS
    lam = (jnp.exp(jnp.sum(lq1_ref[...] * lk1_ref[...], axis=-1, keepdims=True))
           - jnp.exp(jnp.sum(lq2_ref[...] * lk2_ref[...], axis=-1, keepdims=True))
           + lam_init)
    lane = lax.broadcasted_iota(jnp.int32, (1, V7X_LANES), 1)
    causal = (lax.broadcasted_iota(jnp.int32, (tq, tq), 1)
              <= lax.broadcasted_iota(jnp.int32, (tq, tq), 0))
    gain = g_ref[...] * (1.0 - lam_init)
    return lam, lane, lane < DIFF_HEAD_DIM, causal, gain


def _subln_store(o_ref, i, o, gain):
    tq = ATTN_Q_ROWS
    y = o * lax.rsqrt(jnp.mean(o * o, axis=-1, keepdims=True) + LN_EPS) * gain
    o_ref[0, i * tq:(i + 1) * tq, :] = y.astype(_BF16)


def _diff_attn_fast_kernel(lq1_ref, lk1_ref, lq2_ref, lk2_ref, g_ref, q_ref, k_ref, v_ref,
                           o_ref, lmax_ref, k1_ref, k2_ref, vpad_ref, *, lam_init):
    S = q_ref.shape[1]
    tq = ATTN_Q_ROWS
    lam, lane, first, causal, gain = _attn_setup(lq1_ref, lk1_ref, lq2_ref, lk2_ref, g_ref, lam_init)
    neg = jnp.finfo(_F32).min
    in1 = jnp.where(first, 1.0, 0.0)
    free1 = jnp.where(lane == DIFF_HEAD_DIM, 1.0, 0.0)
    free2 = jnp.where(lane == 0, 1.0, 0.0)

    k = k_ref[0].astype(_F32)
    k1_ref[...] = (k * in1 + free1).astype(_BF16)
    k2_ref[...] = (k * (1.0 - in1) + free2).astype(_BF16)
    vpad_ref[:, 0:DIFF_VDIM] = v_ref[0]
    vpad_ref[:, DIFF_VDIM:] = jnp.zeros((S, V7X_LANES), _BF16)

    def exp_scores(i):
        r0, n = i * tq, (i + 1) * tq
        q = q_ref[0, r0:n, :]
        prod = q.astype(_F32) * k_ref[0, r0:n, :].astype(_F32)
        d1 = jnp.sum(jnp.where(first, prod, 0.0), axis=-1, keepdims=True)
        d2 = jnp.sum(jnp.where(first, 0.0, prod), axis=-1, keepdims=True)
        q1 = jnp.where(first, q, (-d1 * free1).astype(_BF16))
        q2 = jnp.where(first, (-d2 * free2).astype(_BF16), q)
        out = []
        for qc, kc_ref in ((q1, k1_ref), (q2, k2_ref)):
            s = lax.dot_general(qc, kc_ref[0:n, :], _NT, preferred_element_type=_F32)
            p = jnp.exp2(jnp.where(causal, s[:, r0:n], neg))
            if i > 0:
                p = jnp.concatenate([jnp.exp2(s[:, 0:r0]), p], axis=1)
            out.append((p.astype(_BF16), jnp.sum(p, axis=-1, keepdims=True)))
        return out

    n_blocks = S // tq
    lmax = jnp.zeros((tq, 1), _F32)
    e_next = exp_scores(0)
    for i in range(n_blocks):
        (p1, l1), (p2, l2) = e_next
        if i + 1 < n_blocks:
            e_next = exp_scores(i + 1)
        n = (i + 1) * tq
        lmax = jnp.maximum(lmax, jnp.maximum(l1, l2))
        a = p1 - (lam * l1 / l2).astype(_BF16) * p2
        kh = n // 2
        o = (jnp.dot(a[:, 0:kh], vpad_ref[0:kh, :], preferred_element_type=_F32)
             + jnp.dot(a[:, kh:n], vpad_ref[kh:n, :], preferred_element_type=_F32))
        _subln_store(o_ref, i, o[:, 0:DIFF_VDIM] * (1.0 / l1), gain)
    lmax_ref[...] = jnp.broadcast_to(jnp.max(lmax, axis=0, keepdims=True), lmax_ref.shape)


def _diff_attn_exact_kernel(lq1_ref, lk1_ref, lq2_ref, lk2_ref, g_ref, q_ref, k_ref, v_ref, o_ref,
                            vaug_ref, *, lam_init):
    S = q_ref.shape[1]
    tq = ATTN_Q_ROWS
    lam, lane, first, causal, gain = _attn_setup(lq1_ref, lk1_ref, lq2_ref, lk2_ref, g_ref, lam_init)
    neg = jnp.finfo(_F32).min

    vaug_ref[:, 0:DIFF_VDIM] = v_ref[0]
    vaug_ref[:, DIFF_VDIM:] = jnp.broadcast_to(jnp.where(lane == 0, 1.0, 0.0), (S, V7X_LANES)).astype(_BF16)

    def scores(i):
        n = (i + 1) * tq
        q = q_ref[0, i * tq:n, :]
        zero = jnp.zeros_like(q)
        return [lax.dot_general(qc, k_ref[0, 0:n, :], _NT, preferred_element_type=_F32)
                for qc in (jnp.where(first, q, zero), jnp.where(first, zero, q))]

    def weighted_values(s, i):
        r0, n = i * tq, (i + 1) * tq
        s_d = jnp.where(causal, s[:, r0:n], neg)
        m = jnp.max(s_d, axis=-1, keepdims=True)
        if i > 0:
            m = jnp.maximum(m, jnp.max(s[:, 0:r0], axis=-1, keepdims=True))
            p = jnp.concatenate([jnp.exp2(s[:, 0:r0] - m), jnp.exp2(s_d - m)], axis=1)
        else:
            p = jnp.exp2(s_d - m)
        return jnp.dot(p.astype(_BF16), vaug_ref[0:n, :], preferred_element_type=_F32)

    n_blocks = S // tq
    s_next = scores(0)
    for i in range(n_blocks):
        s_cur = s_next
        if i + 1 < n_blocks:
            s_next = scores(i + 1)
        acc1 = weighted_values(s_cur[0], i)
        acc2 = weighted_values(s_cur[1], i)
        r1 = 1.0 / acc1[:, DIFF_VDIM:DIFF_VDIM + 1]
        r2 = lam / acc2[:, DIFF_VDIM:DIFF_VDIM + 1]
        _subln_store(o_ref, i, acc1[:, 0:DIFF_VDIM] * r1 - acc2[:, 0:DIFF_VDIM] * r2, gain)


def _diff_attn(q, k, v, lq1, lk1, lq2, lk2, subln_g, lam_init):
    B, S, _ = q.shape
    head = pl.BlockSpec((1, S, DIFF_VDIM), lambda b, h: (b, 0, h))
    vec = _resident((1, DIFF_HEAD_DIM))
    in_specs = [vec, vec, vec, vec, _resident((1, DIFF_VDIM)), head, head, head]
    params = pltpu.CompilerParams(dimension_semantics=("arbitrary", "arbitrary"),
                                  vmem_limit_bytes=V7X_VMEM_LIMIT_BYTES)
    out_sd = jax.ShapeDtypeStruct((B, S, DIFF_WIDTH), _BF16)
    args = (lq1, lk1, lq2, lk2, subln_g, q, k, v)
    wide = DIFF_VDIM + V7X_LANES

    fast, lmax = pl.pallas_call(
        partial(_diff_attn_fast_kernel, lam_init=lam_init),
        grid=(B, N_DIFF_HEADS),
        in_specs=in_specs,
        out_specs=[head, pl.BlockSpec((1, 1, 8, V7X_LANES), lambda b, h: (b, h, 0, 0))],
        out_shape=[out_sd, jax.ShapeDtypeStruct((B, N_DIFF_HEADS, 8, V7X_LANES), _F32)],
        scratch_shapes=[pltpu.VMEM((S, DIFF_VDIM), _BF16), pltpu.VMEM((S, DIFF_VDIM), _BF16),
                        pltpu.VMEM((S, wide), _BF16)],
        compiler_params=params,
        name="diff_attn_fast",
    )(*args)

    def exact():
        return pl.pallas_call(
            partial(_diff_attn_exact_kernel, lam_init=lam_init),
            grid=(B, N_DIFF_HEADS),
            in_specs=in_specs,
            out_specs=head,
            out_shape=out_sd,
            scratch_shapes=[pltpu.VMEM((S, wide), _BF16)],
            compiler_params=params,
            name="diff_attn_exact",
        )(*args)

    in_range = jnp.max(lmax) <= ATTN_ROWSUM_LIMIT
    return lax.cond(in_range, lambda: fast, exact)


def _out_ffn_kernel(x_ref, attn_ref, sgu_ref, gate1_ref, shift2_ref, scale2_ref, gate2_ref,
                    wo_ref, ln1g_ref, ln1b_ref, wg_ref, wu_ref, wd_ref, ln2g_ref, ln2b_ref,
                    o_ref):
    rows = x_ref.shape[1]
    b = pl.program_id(0)
    gate1, shift2, scale2, gate2 = [r[pl.ds(b, 1), :]
                                    for r in (gate1_ref, shift2_ref, scale2_ref, gate2_ref)]
    slabs = [slice(r, r + FFN_SLAB_ROWS) for r in range(0, rows, FFN_SLAB_ROWS)]
    mix = [jnp.dot(attn_ref[0, rs, :], wo_ref[0:DIFF_WIDTH, :], preferred_element_type=_F32)
           + jnp.dot(sgu_ref[0, rs, :], wo_ref[DIFF_WIDTH:, :], preferred_element_type=_F32)
           for rs in slabs]
    for rs, mix_s in zip(slabs, mix):
        x1 = _layernorm_rows(ALPHA * x_ref[0, rs, :] + gate1 * mix_s, ln1g_ref[...], ln1b_ref[...])
        h2 = (x1 * (1.0 + scale2) + shift2).astype(_BF16)
        g = jnp.dot(h2, wg_ref[...], preferred_element_type=_F32)
        u = jnp.dot(h2, wu_ref[...], preferred_element_type=_F32)
        a = (g * jax.nn.sigmoid(g) * u).astype(_BF16)
        ffn = jnp.dot(a, wd_ref[...], preferred_element_type=_F32)
        o_ref[0, rs, :] = _layernorm_rows(ALPHA * x1 + gate2 * ffn, ln2g_ref[...], ln2b_ref[...])


def _out_ffn(x, attn, sgu, mod, w_o, ln1_g, ln1_b, w_gate, w_up, w_down, ln2_g, ln2_b):
    B, S, D = x.shape
    F = w_gate.shape[1]
    rows = FFN_ROWS
    tok = lambda width: pl.BlockSpec((1, rows, width), lambda b, i: (b, i, 0))
    vec = _resident((1, D))
    return pl.pallas_call(
        _out_ffn_kernel,
        grid=(B, S // rows),
        in_specs=[
            tok(D), tok(DIFF_WIDTH), tok(SGU_WIDTH),
            _mod_spec(B, D, MOD_GATE1), _mod_spec(B, D, MOD_SHIFT2),
            _mod_spec(B, D, MOD_SCALE2), _mod_spec(B, D, MOD_GATE2),
            _resident((D, D)), vec, vec, _resident((D, F)), _resident((D, F)), _resident((F, D)),
            vec, vec,
        ],
        out_specs=tok(D),
        out_shape=jax.ShapeDtypeStruct((B, S, D), x.dtype),
        compiler_params=pltpu.CompilerParams(
            dimension_semantics=("arbitrary", "arbitrary"),
            vmem_limit_bytes=V7X_VMEM_LIMIT_BYTES),
        name="out_ffn",
    )(x, attn, sgu, mod, mod, mod, mod, w_o, ln1_g, ln1_b, w_gate, w_up, w_down, ln2_g, ln2_b)


def _rope_tables(seq, scale):
    half = ROT_DIM // 2
    inv_freq = np.float32(ROPE_THETA) ** (-np.arange(half, dtype=np.float32) * np.float32(2.0 / ROT_DIM))
    ang = np.arange(seq, dtype=np.float32)[:, None] * inv_freq[None, :]
    cos, sin = np.cos(ang), np.sin(ang)
    pad = np.zeros((seq, DIFF_HEAD_DIM - ROT_DIM), np.float32)
    zer = np.zeros((seq, half), np.float32)
    c_tab = np.concatenate([cos, cos, pad + 1.0], axis=-1)
    s_lo = np.concatenate([-sin, zer, pad], axis=-1)
    s_hi = np.concatenate([zer, sin, pad], axis=-1)
    tabs = np.stack([c_tab, s_lo, s_hi]).astype(np.float32) * np.float32(scale)
    return jnp.asarray(np.tile(tabs, (1, 1, V7X_LANES // DIFF_HEAD_DIM)))


def kernel(x, c, ada_w, ada_b, w_in, lambda_q1, lambda_k1, lambda_q2, lambda_k2, subln_g,
           sgu_ln_g, sgu_ln_b, sgu_w, sgu_b, w_o, ln1_g, ln1_b, w_gate, w_up, w_down,
           ln2_g, ln2_b):
    B, S, D = x.shape
    rope_q = _rope_tables(S, DIFF_HEAD_DIM ** -0.5 * math.log2(math.e))
    rope_k = _rope_tables(S, 1.0)
    for l in range(DEPTH):
        lam_init = 0.8 - 0.6 * math.exp(-0.3 * l)
        mod, w_in16 = _adaln_mod(c, ada_w[l], ada_b[l][None, :], w_in[l])
        (q, k, v, sgu), (w_o16, w_gate16, w_up16, w_down16) = _in_proj(
            x, mod, w_in16, rope_q, rope_k,
            sgu_ln_g[l][None, :], sgu_ln_b[l][None, :], sgu_w[l], sgu_b[l].T,
            (w_o[l], w_gate[l], w_up[l], w_down[l]))
        attn = _diff_attn(q, k, v, lambda_q1[l][None, :], lambda_k1[l][None, :],
                          lambda_q2[l][None, :], lambda_k2[l][None, :],
                          subln_g[l][None, :], lam_init)
        x = _out_ffn(x, attn, sgu, mod, w_o16, ln1_g[l][None, :], ln1_b[l][None, :],
                     w_gate16, w_up16, w_down16, ln2_g[l][None, :], ln2_b[l][None, :])
    return x
```

```python
import math
from functools import partial

import numpy as np
import jax
import jax.numpy as jnp
from jax import lax
from jax.experimental import pallas as pl
from jax.experimental.pallas import tpu as pltpu

D_MODEL = 1024
N_DIFF_HEADS = 4
DIFF_HEAD_DIM = 64
DIFF_VDIM = 2 * DIFF_HEAD_DIM
DIFF_WIDTH = N_DIFF_HEADS * DIFF_VDIM
SGU_WIDTH = D_MODEL - DIFF_WIDTH
SGU_GROUPS = 4
SGU_GROUP_DIM = SGU_WIDTH // SGU_GROUPS
CHUNK = 128
ROT_DIM = DIFF_HEAD_DIM // 4
ROPE_THETA = 500000.0
DEPTH = 1
ALPHA = (2 * DEPTH) ** 0.25
LN_EPS = 1e-5
QKV_COLS = DIFF_WIDTH
MOD_SHIFT1, MOD_SCALE1, MOD_GATE1, MOD_SHIFT2, MOD_SCALE2, MOD_GATE2 = range(6)

V7X_LANES = 128
V7X_BF16_SUBLANES = 16
V7X_VMEM_LIMIT_BYTES = 56 * 1024 * 1024

PROJ_ROWS = 1024
FFN_ROWS = 1024
FFN_SLAB_ROWS = 256
ATTN_Q_ROWS = 256
ATTN_FAST_HEADS_PER_STEP = 2
ATTN_ROWSUM_LIMIT = 2.0 ** 60
ADA_COLS = 768

_BF16 = jnp.bfloat16
_F32 = jnp.float32
_NT = (((1,), (1,)), ((), ()))


def _resident(shape):
    zeros = (0,) * len(shape)
    return pl.BlockSpec(shape, lambda *_: zeros, pipeline_mode=pl.Buffered(1))


def _layernorm_rows(y, g, b):
    mu = jnp.mean(y, axis=-1, keepdims=True)
    d = y - mu
    var = jnp.mean(d * d, axis=-1, keepdims=True)
    return d * lax.rsqrt(var + LN_EPS) * g + b


def _adaln_kernel(c_ref, w_ref, b_ref, win_ref, o_ref, win16_ref):
    c = c_ref[...]
    c_act = (c * jax.nn.sigmoid(c)).astype(_BF16)
    acc = jnp.dot(c_act, w_ref[...].astype(_BF16), preferred_element_type=_F32)
    o_ref[...] = acc + b_ref[...]
    win16_ref[...] = win_ref[...].astype(_BF16)


def _adaln_mod(c, ada_w, ada_b, w_in):
    B, D = c.shape
    N = ada_w.shape[1]
    steps = N // ADA_COLS
    win_spec = pl.BlockSpec((w_in.shape[0] // steps, w_in.shape[1]), lambda j: (j, 0))
    return pl.pallas_call(
        _adaln_kernel,
        grid=(steps,),
        in_specs=[
            pl.BlockSpec((B, D), lambda j: (0, 0)),
            pl.BlockSpec((D, ADA_COLS), lambda j: (0, j)),
            pl.BlockSpec((1, ADA_COLS), lambda j: (0, j)),
            win_spec,
        ],
        out_specs=[pl.BlockSpec((B, ADA_COLS), lambda j: (0, j)), win_spec],
        out_shape=[jax.ShapeDtypeStruct((B, N), _F32), jax.ShapeDtypeStruct(w_in.shape, _BF16)],
        compiler_params=pltpu.CompilerParams(
            dimension_semantics=("arbitrary",), vmem_limit_bytes=V7X_VMEM_LIMIT_BYTES),
        name="adaln_mod",
    )(c, ada_w, ada_b, w_in)


def _rope(t, c_tab, s_lo, s_hi):
    up = pltpu.roll(t, V7X_LANES - ROT_DIM // 2, 1)
    dn = pltpu.roll(t, ROT_DIM // 2, 1)
    return t * c_tab + up * s_lo + dn * s_hi


def _in_proj_kernel(x_ref, scale_ref, shift_ref, w_ref, ropeq_ref, ropek_ref,
                    lng_ref, lnb_ref, sw_ref, sbt_ref, *rest):
    n_cast = (len(rest) - 4) // 2
    cast_in, (q_ref, k_ref, v_ref, sgu_ref), cast_out = (
        rest[:n_cast], rest[n_cast:n_cast + 4], rest[n_cast + 4:])
    for src, dst in zip(cast_in, cast_out):
        dst[...] = src[...].astype(_BF16)

    rows = x_ref.shape[1]
    n_chunks = rows // CHUNK
    b = pl.program_id(0)
    scale, shift = scale_ref[pl.ds(b, 1), :], shift_ref[pl.ds(b, 1), :]
    h = (x_ref[0] * (1.0 + scale) + shift).astype(_BF16)

    def proj(lo, width):
        return jnp.dot(h, w_ref[:, lo:lo + width], preferred_element_type=_F32)

    o3 = 3 * QKV_COLS
    zu = proj(o3, SGU_WIDTH)
    zv = proj(o3 + SGU_WIDTH, SGU_WIDTH)
    inv_sqrt2 = 1.0 / math.sqrt(2.0)
    u = 0.5 * zu * (1.0 + lax.erf(zu * inv_sqrt2))
    vv = 0.5 * zv * (1.0 + lax.erf(zv * inv_sqrt2))
    vv = _layernorm_rows(vv, lng_ref[...], lnb_ref[...]).astype(_BF16)

    for out_ref, tabs, base in ((q_ref, ropeq_ref, 0), (k_ref, ropek_ref, QKV_COLS)):
        c_tab, s_lo, s_hi = tabs[0], tabs[1], tabs[2]
        t = proj(base, QKV_COLS)
        for s in range(QKV_COLS // V7X_LANES):
            sl = slice(s * V7X_LANES, (s + 1) * V7X_LANES)
            out_ref[0, :, sl] = _rope(t[:, sl], c_tab, s_lo, s_hi).astype(_BF16)

    v_ref[0] = proj(2 * QKV_COLS, QKV_COLS).astype(_BF16)

    tri = (lax.broadcasted_iota(jnp.int32, (CHUNK, CHUNK), 0)
           >= lax.broadcasted_iota(jnp.int32, (CHUNK, CHUNK), 1))
    for g in range(SGU_GROUPS):
        cols = slice(g * SGU_GROUP_DIM, (g + 1) * SGU_GROUP_DIM)
        w_g = jnp.where(tri, sw_ref[g], 0.0).astype(_BF16)
        b_g = sbt_ref[:, g:g + 1]
        v_g = jnp.concatenate([vv[n * CHUNK:(n + 1) * CHUNK, cols] for n in range(n_chunks)], axis=1)
        sv = jnp.dot(w_g, v_g, preferred_element_type=_F32) + b_g
        for n in range(n_chunks):
            rws = slice(n * CHUNK, (n + 1) * CHUNK)
            sgu_ref[0, rws, cols] = (
                u[rws, cols] * sv[:, n * SGU_GROUP_DIM:(n + 1) * SGU_GROUP_DIM]).astype(_BF16)


def _row_block_spec(n_rows, n_cols, n_steps, steps_per_batch):
    n_blocks = n_steps
    while n_rows % (n_blocks * V7X_BF16_SUBLANES):
        n_blocks //= 2
    repeat = n_steps // n_blocks
    return pl.BlockSpec((n_rows // n_blocks, n_cols),
                        lambda b, i: ((b * steps_per_batch + i) // repeat, 0))


def _mod_spec(B, D, which):
    return pl.BlockSpec((B, D), lambda b, i: (0, which))


def _in_proj(x, mod, w_in, rope_q, rope_k, sgu_ln_g, sgu_ln_b, sgu_w, sgu_bt, later_weights):
    B, S, D = x.shape
    P = w_in.shape[1]
    rows = PROJ_ROWS
    steps_per_batch = S // rows
    tok = lambda width: pl.BlockSpec((1, rows, width), lambda b, i: (b, i, 0))
    rope_spec = pl.BlockSpec((3, rows, V7X_LANES), lambda b, i: (0, i, 0))
    out_sd = jax.ShapeDtypeStruct((B, S, QKV_COLS), _BF16)
    cast_specs = [_row_block_spec(w.shape[0], w.shape[1], B * steps_per_batch, steps_per_batch)
                  for w in later_weights]
    outs = pl.pallas_call(
        _in_proj_kernel,
        grid=(B, steps_per_batch),
        in_specs=[
            tok(D), _mod_spec(B, D, MOD_SCALE1), _mod_spec(B, D, MOD_SHIFT1),
            _resident((D, P)), rope_spec, rope_spec,
            _resident((1, SGU_WIDTH)), _resident((1, SGU_WIDTH)),
            _resident((SGU_GROUPS, CHUNK, CHUNK)), _resident((CHUNK, SGU_GROUPS)),
        ] + cast_specs,
        out_specs=[tok(QKV_COLS), tok(QKV_COLS), tok(QKV_COLS), tok(SGU_WIDTH)] + cast_specs,
        out_shape=[out_sd, out_sd, out_sd, jax.ShapeDtypeStruct((B, S, SGU_WIDTH), _BF16)]
        + [jax.ShapeDtypeStruct(w.shape, _BF16) for w in later_weights],
        compiler_params=pltpu.CompilerParams(
            dimension_semantics=("arbitrary", "arbitrary"),
            vmem_limit_bytes=V7X_VMEM_LIMIT_BYTES),
        name="in_proj",
    )(x, mod, mod, w_in, rope_q, rope_k, sgu_ln_g, sgu_ln_b, sgu_w, sgu_bt, *later_weights)
    return outs[:4], outs[4:]


def _attn_setup(lq1_ref, lk1_ref, lq2_ref, lk2_ref, g_ref, lam_init):
    tq = ATTN_Q_ROWS
    lam = (jnp.exp(jnp.sum(lq1_ref[...] * lk1_ref[...], axis=-1, keepdims=True))
           - jnp.exp(jnp.sum(lq2_ref[...] * lk2_ref[...], axis=-1, keepdims=True))
           + lam_init)
    lane = lax.broadcasted_iota(jnp.int32, (1, V7X_LANES), 1)
    causal = (lax.broadcasted_iota(jnp.int32, (tq, tq), 1)
              <= lax.broadcasted_iota(jnp.int32, (tq, tq), 0))
    gain = g_ref[...] * (1.0 - lam_init)
    return lam, lane, lane < DIFF_HEAD_DIM, causal, gain


def _subln_store(o_ref, i, h, o, gain):
    tq = ATTN_Q_ROWS
    y = o * lax.rsqrt(jnp.mean(o * o, axis=-1, keepdims=True) + LN_EPS) * gain
    o_ref[0, i * tq:(i + 1) * tq, h * DIFF_VDIM:(h + 1) * DIFF_VDIM] = y.astype(_BF16)


def _diff_attn_fast_kernel(lq1_ref, lk1_ref, lq2_ref, lk2_ref, g_ref, q_ref, k_ref, v_ref,
                           o_ref, lrange_ref, vpad_ref, *, lam_init):
    S = q_ref.shape[1]
    n_heads = q_ref.shape[2] // DIFF_VDIM
    wide = DIFF_VDIM + V7X_LANES
    tq = ATTN_Q_ROWS
    lam, _, first, causal, gain = _attn_setup(lq1_ref, lk1_ref, lq2_ref, lk2_ref, g_ref, lam_init)
    neg = jnp.finfo(_F32).min

    for h in range(n_heads):
        vpad_ref[:, h * wide:h * wide + DIFF_VDIM] = v_ref[0, :, h * DIFF_VDIM:(h + 1) * DIFF_VDIM]
        vpad_ref[:, h * wide + DIFF_VDIM:(h + 1) * wide] = jnp.zeros((S, V7X_LANES), _BF16)

    def exp_scores(h, i):
        r0, n = i * tq, (i + 1) * tq
        cols = slice(h * DIFF_VDIM, (h + 1) * DIFF_VDIM)
        q = q_ref[0, r0:n, cols]
        zero = jnp.zeros_like(q)
        out = []
        for qc in (jnp.where(first, q, zero), jnp.where(first, zero, q)):
            s = lax.dot_general(qc, k_ref[0, 0:n, cols], _NT, preferred_element_type=_F32)
            p = jnp.exp2(jnp.where(causal, s[:, r0:n], neg))
            if i > 0:
                p = jnp.concatenate([jnp.exp2(s[:, 0:r0]), p], axis=1)
            out.append((p.astype(_BF16), jnp.sum(p, axis=-1, keepdims=True)))
        return out

    stages = [(h, i) for h in range(n_heads) for i in range(S // tq)]
    lmax = jnp.zeros((tq, 1), _F32)
    lmin = jnp.full((tq, 1), jnp.inf, _F32)
    e_next = exp_scores(*stages[0])
    for idx, (h, i) in enumerate(stages):
        (p1, l1), (p2, l2) = e_next
        if idx + 1 < len(stages):
            e_next = exp_scores(*stages[idx + 1])
        n = (i + 1) * tq
        lmax = jnp.maximum(lmax, jnp.maximum(l1, l2))
        lmin = jnp.minimum(lmin, jnp.minimum(l1, l2))
        a = p1 - (lam * l1 / l2).astype(_BF16) * p2
        kh = n // 2
        vh = slice(h * wide, (h + 1) * wide)
        o = (jnp.dot(a[:, 0:kh], vpad_ref[0:kh, vh], preferred_element_type=_F32)
             + jnp.dot(a[:, kh:n], vpad_ref[kh:n, vh], preferred_element_type=_F32))
        _subln_store(o_ref, i, h, o[:, 0:DIFF_VDIM] * (1.0 / l1), gain)
    top = jnp.max(lmax, axis=0, keepdims=True)
    bot = jnp.min(lmin, axis=0, keepdims=True)
    upper = lax.broadcasted_iota(jnp.int32, (8, V7X_LANES), 0) < 4
    lrange_ref[0, 0] = jnp.where(upper, top, bot)


def _diff_attn_exact_kernel(lq1_ref, lk1_ref, lq2_ref, lk2_ref, g_ref, q_ref, k_ref, v_ref, o_ref,
                            vaug_ref, *, lam_init):
    S = q_ref.shape[1]
    tq = ATTN_Q_ROWS
    lam, lane, first, causal, gain = _attn_setup(lq1_ref, lk1_ref, lq2_ref, lk2_ref, g_ref, lam_init)
    neg = jnp.finfo(_F32).min

    vaug_ref[:, 0:DIFF_VDIM] = v_ref[0]
    vaug_ref[:, DIFF_VDIM:] = jnp.broadcast_to(jnp.where(lane == 0, 1.0, 0.0), (S, V7X_LANES)).astype(_BF16)

    def scores(i):
        n = (i + 1) * tq
        q = q_ref[0, i * tq:n, :]
        zero = jnp.zeros_like(q)
        return [lax.dot_general(qc, k_ref[0, 0:n, :], _NT, preferred_element_type=_F32)
                for qc in (jnp.where(first, q, zero), jnp.where(first, zero, q))]

    def weighted_values(s, i):
        r0, n = i * tq, (i + 1) * tq
        s_d = jnp.where(causal, s[:, r0:n], neg)
        m = jnp.max(s_d, axis=-1, keepdims=True)
        if i > 0:
            m = jnp.maximum(m, jnp.max(s[:, 0:r0], axis=-1, keepdims=True))
            p = jnp.concatenate([jnp.exp2(s[:, 0:r0] - m), jnp.exp2(s_d - m)], axis=1)
        else:
            p = jnp.exp2(s_d - m)
        return jnp.dot(p.astype(_BF16), vaug_ref[0:n, :], preferred_element_type=_F32)

    n_blocks = S // tq
    s_next = scores(0)
    for i in range(n_blocks):
        s_cur = s_next
        if i + 1 < n_blocks:
            s_next = scores(i + 1)
        acc1 = weighted_values(s_cur[0], i)
        acc2 = weighted_values(s_cur[1], i)
        r1 = 1.0 / acc1[:, DIFF_VDIM:DIFF_VDIM + 1]
        r2 = lam / acc2[:, DIFF_VDIM:DIFF_VDIM + 1]
        _subln_store(o_ref, i, 0, acc1[:, 0:DIFF_VDIM] * r1 - acc2[:, 0:DIFF_VDIM] * r2, gain)


def _diff_attn(q, k, v, lq1, lk1, lq2, lk2, subln_g, lam_init):
    B, S, _ = q.shape
    head = pl.BlockSpec((1, S, DIFF_VDIM), lambda b, h: (b, 0, h))
    vec = _resident((1, DIFF_HEAD_DIM))
    in_specs = [vec, vec, vec, vec, _resident((1, DIFF_VDIM)), head, head, head]
    params = pltpu.CompilerParams(dimension_semantics=("arbitrary", "arbitrary"),
                                  vmem_limit_bytes=V7X_VMEM_LIMIT_BYTES)
    out_sd = jax.ShapeDtypeStruct((B, S, DIFF_WIDTH), _BF16)
    args = (lq1, lk1, lq2, lk2, subln_g, q, k, v)
    wide = DIFF_VDIM + V7X_LANES

    hps = ATTN_FAST_HEADS_PER_STEP
    heads = pl.BlockSpec((1, S, hps * DIFF_VDIM), lambda b, h: (b, 0, h))
    fast, lrange = pl.pallas_call(
        partial(_diff_attn_fast_kernel, lam_init=lam_init),
        grid=(B, N_DIFF_HEADS // hps),
        in_specs=in_specs[:5] + [heads] * 3,
        out_specs=[heads, pl.BlockSpec((1, 1, 8, V7X_LANES), lambda b, h: (b, h, 0, 0))],
        out_shape=[out_sd, jax.ShapeDtypeStruct((B, N_DIFF_HEADS // hps, 8, V7X_LANES), _F32)],
        scratch_shapes=[pltpu.VMEM((S, hps * wide), _BF16)],
        compiler_params=params,
        name="diff_attn_fast",
    )(*args)

    def exact():
        return pl.pallas_call(
            partial(_diff_attn_exact_kernel, lam_init=lam_init),
            grid=(B, N_DIFF_HEADS),
            in_specs=in_specs,
            out_specs=head,
            out_shape=out_sd,
            scratch_shapes=[pltpu.VMEM((S, wide), _BF16)],
            compiler_params=params,
            name="diff_attn_exact",
        )(*args)

    in_range = ((jnp.max(lrange[:, :, 0, 0]) <= ATTN_ROWSUM_LIMIT)
                & (jnp.min(lrange[:, :, 4, 0]) >= 1.0 / ATTN_ROWSUM_LIMIT))
    return lax.cond(in_range, lambda: fast, exact)


def _out_ffn_kernel(x_ref, attn_ref, sgu_ref, gate1_ref, shift2_ref, scale2_ref, gate2_ref,
                    wo_ref, ln1g_ref, ln1b_ref, wg_ref, wu_ref, wd_ref, ln2g_ref, ln2b_ref,
                    o_ref):
    rows = x_ref.shape[1]
    b = pl.program_id(0)
    gate1, shift2, scale2, gate2 = [r[pl.ds(b, 1), :]
                                    for r in (gate1_ref, shift2_ref, scale2_ref, gate2_ref)]
    slabs = [slice(r, r + FFN_SLAB_ROWS) for r in range(0, rows, FFN_SLAB_ROWS)]
    mix = [jnp.dot(attn_ref[0, rs, :], wo_ref[0:DIFF_WIDTH, :], preferred_element_type=_F32)
           + jnp.dot(sgu_ref[0, rs, :], wo_ref[DIFF_WIDTH:, :], preferred_element_type=_F32)
           for rs in slabs]
    for rs, mix_s in zip(slabs, mix):
        x1 = _layernorm_rows(ALPHA * x_ref[0, rs, :] + gate1 * mix_s, ln1g_ref[...], ln1b_ref[...])
        h2 = (x1 * (1.0 + scale2) + shift2).astype(_BF16)
        g = jnp.dot(h2, wg_ref[...], preferred_element_type=_F32)
        u = jnp.dot(h2, wu_ref[...], preferred_element_type=_F32)
        a = (g * jax.nn.sigmoid(g) * u).astype(_BF16)
        ffn = jnp.dot(a, wd_ref[...], preferred_element_type=_F32)
        o_ref[0, rs, :] = _layernorm_rows(ALPHA * x1 + gate2 * ffn, ln2g_ref[...], ln2b_ref[...])


def _out_ffn(x, attn, sgu, mod, w_o, ln1_g, ln1_b, w_gate, w_up, w_down, ln2_g, ln2_b):
    B, S, D = x.shape
    F = w_gate.shape[1]
    rows = FFN_ROWS
    tok = lambda width: pl.BlockSpec((1, rows, width), lambda b, i: (b, i, 0))
    vec = _resident((1, D))
    return pl.pallas_call(
        _out_ffn_kernel,
        grid=(B, S // rows),
        in_specs=[
            tok(D), tok(DIFF_WIDTH), tok(SGU_WIDTH),
            _mod_spec(B, D, MOD_GATE1), _mod_spec(B, D, MOD_SHIFT2),
            _mod_spec(B, D, MOD_SCALE2), _mod_spec(B, D, MOD_GATE2),
            _resident((D, D)), vec, vec, _resident((D, F)), _resident((D, F)), _resident((F, D)),
            vec, vec,
        ],
        out_specs=tok(D),
        out_shape=jax.ShapeDtypeStruct((B, S, D), x.dtype),
        compiler_params=pltpu.CompilerParams(
            dimension_semantics=("arbitrary", "arbitrary"),
            vmem_limit_bytes=V7X_VMEM_LIMIT_BYTES),
        name="out_ffn",
    )(x, attn, sgu, mod, mod, mod, mod, w_o, ln1_g, ln1_b, w_gate, w_up, w_down, ln2_g, ln2_b)


def _rope_tables(seq, scale):
    half = ROT_DIM // 2
    inv_freq = np.float32(ROPE_THETA) ** (-np.arange(half, dtype=np.float32) * np.float32(2.0 / ROT_DIM))
    ang = np.arange(seq, dtype=np.float32)[:, None] * inv_freq[None, :]
    cos, sin = np.cos(ang), np.sin(ang)
    pad = np.zeros((seq, DIFF_HEAD_DIM - ROT_DIM), np.float32)
    zer = np.zeros((seq, half), np.float32)
    c_tab = np.concatenate([cos, cos, pad + 1.0], axis=-1)
    s_lo = np.concatenate([-sin, zer, pad], axis=-1)
    s_hi = np.concatenate([zer, sin, pad], axis=-1)
    tabs = np.stack([c_tab, s_lo, s_hi]).astype(np.float32) * np.float32(scale)
    return jnp.asarray(np.tile(tabs, (1, 1, V7X_LANES // DIFF_HEAD_DIM)))


def kernel(x, c, ada_w, ada_b, w_in, lambda_q1, lambda_k1, lambda_q2, lambda_k2, subln_g,
           sgu_ln_g, sgu_ln_b, sgu_w, sgu_b, w_o, ln1_g, ln1_b, w_gate, w_up, w_down,
           ln2_g, ln2_b):
    B, S, D = x.shape
    rope_q = _rope_tables(S, DIFF_HEAD_DIM ** -0.5 * math.log2(math.e))
    rope_k = _rope_tables(S, 1.0)
    for l in range(DEPTH):
        lam_init = 0.8 - 0.6 * math.exp(-0.3 * l)
        mod, w_in16 = _adaln_mod(c, ada_w[l], ada_b[l][None, :], w_in[l])
        (q, k, v, sgu), (w_o16, w_gate16, w_up16, w_down16) = _in_proj(
            x, mod, w_in16, rope_q, rope_k,
            sgu_ln_g[l][None, :], sgu_ln_b[l][None, :], sgu_w[l], sgu_b[l].T,
            (w_o[l], w_gate[l], w_up[l], w_down[l]))
        attn = _diff_attn(q, k, v, lambda_q1[l][None, :], lambda_k1[l][None, :],
                          lambda_q2[l][None, :], lambda_k2[l][None, :],
                          subln_g[l][None, :], lam_init)
        x = _out_ffn(x, attn, sgu, mod, w_o16, ln1_g[l][None, :], ln1_b[l][None, :],
                     w_gate16, w_up16, w_down16, ln2_g[l][None, :], ln2_b[l][None, :])
    return x
```

```python
import math
from functools import partial

import numpy as np
import jax
import jax.numpy as jnp
from jax import lax
from jax.experimental import pallas as pl
from jax.experimental.pallas import tpu as pltpu

D_MODEL = 1024
N_DIFF_HEADS = 4
DIFF_HEAD_DIM = 64
DIFF_VDIM = 2 * DIFF_HEAD_DIM
DIFF_WIDTH = N_DIFF_HEADS * DIFF_VDIM
SGU_WIDTH = D_MODEL - DIFF_WIDTH
SGU_GROUPS = 4
SGU_GROUP_DIM = SGU_WIDTH // SGU_GROUPS
CHUNK = 128
ROT_DIM = DIFF_HEAD_DIM // 4
ROPE_THETA = 500000.0
DEPTH = 1
ALPHA = (2 * DEPTH) ** 0.25
LN_EPS = 1e-5
QKV_COLS = DIFF_WIDTH
MOD_SHIFT1, MOD_SCALE1, MOD_GATE1, MOD_SHIFT2, MOD_SCALE2, MOD_GATE2 = range(6)

V7X_LANES = 128
V7X_BF16_SUBLANES = 16
V7X_VMEM_LIMIT_BYTES = 56 * 1024 * 1024

PROJ_ROWS = 1024
FFN_ROWS = 1024
FFN_SLAB_ROWS = 256
ATTN_Q_ROWS = 256
ATTN_FAST_HEADS_PER_STEP = 2
ATTN_ROWSUM_LIMIT = 2.0 ** 60
ADA_COLS = 3072

_BF16 = jnp.bfloat16
_F32 = jnp.float32
_NT = (((1,), (1,)), ((), ()))


def _resident(shape):
    zeros = (0,) * len(shape)
    return pl.BlockSpec(shape, lambda *_: zeros, pipeline_mode=pl.Buffered(1))


def _layernorm_rows(y, g, b):
    mu = jnp.mean(y, axis=-1, keepdims=True)
    d = y - mu
    var = jnp.mean(d * d, axis=-1, keepdims=True)
    return d * lax.rsqrt(var + LN_EPS) * g + b


def _adaln_kernel(c_ref, w_ref, b_ref, win_ref, o_ref, win16_ref):
    c = c_ref[...]
    c_act = (c * jax.nn.sigmoid(c)).astype(_BF16)
    acc = jnp.dot(c_act, w_ref[...].astype(_BF16), preferred_element_type=_F32)
    o_ref[...] = acc + b_ref[...]
    win16_ref[...] = win_ref[...].astype(_BF16)


def _adaln_mod(c, ada_w, ada_b, w_in):
    B, D = c.shape
    N = ada_w.shape[1]
    steps = N // ADA_COLS
    win_spec = pl.BlockSpec((w_in.shape[0] // steps, w_in.shape[1]), lambda j: (j, 0))
    return pl.pallas_call(
        _adaln_kernel,
        grid=(steps,),
        in_specs=[
            pl.BlockSpec((B, D), lambda j: (0, 0)),
            pl.BlockSpec((D, ADA_COLS), lambda j: (0, j)),
            pl.BlockSpec((1, ADA_COLS), lambda j: (0, j)),
            win_spec,
        ],
        out_specs=[pl.BlockSpec((B, ADA_COLS), lambda j: (0, j)), win_spec],
        out_shape=[jax.ShapeDtypeStruct((B, N), _F32), jax.ShapeDtypeStruct(w_in.shape, _BF16)],
        compiler_params=pltpu.CompilerParams(
            dimension_semantics=("arbitrary",), vmem_limit_bytes=V7X_VMEM_LIMIT_BYTES),
        name="adaln_mod",
    )(c, ada_w, ada_b, w_in)


def _rope(t, c_tab, s_lo, s_hi):
    up = pltpu.roll(t, V7X_LANES - ROT_DIM // 2, 1)
    dn = pltpu.roll(t, ROT_DIM // 2, 1)
    return t * c_tab + up * s_lo + dn * s_hi


def _in_proj_kernel(x_ref, scale_ref, shift_ref, w_ref, ropeq_ref, ropek_ref,
                    lng_ref, lnb_ref, sw_ref, sbt_ref, *rest):
    n_cast = (len(rest) - 4) // 2
    cast_in, (q_ref, k_ref, v_ref, sgu_ref), cast_out = (
        rest[:n_cast], rest[n_cast:n_cast + 4], rest[n_cast + 4:])
    for src, dst in zip(cast_in, cast_out):
        dst[...] = src[...].astype(_BF16)

    rows = x_ref.shape[1]
    n_chunks = rows // CHUNK
    b = pl.program_id(0)
    scale, shift = scale_ref[pl.ds(b, 1), :], shift_ref[pl.ds(b, 1), :]
    h = (x_ref[0] * (1.0 + scale) + shift).astype(_BF16)

    def proj(lo, width):
        return jnp.dot(h, w_ref[:, lo:lo + width], preferred_element_type=_F32)

    o3 = 3 * QKV_COLS
    zu = proj(o3, SGU_WIDTH)
    zv = proj(o3 + SGU_WIDTH, SGU_WIDTH)
    inv_sqrt2 = 1.0 / math.sqrt(2.0)
    u = 0.5 * zu * (1.0 + lax.erf(zu * inv_sqrt2))
    vv = 0.5 * zv * (1.0 + lax.erf(zv * inv_sqrt2))
    vv = _layernorm_rows(vv, lng_ref[...], lnb_ref[...]).astype(_BF16)

    for out_ref, tabs, base in ((q_ref, ropeq_ref, 0), (k_ref, ropek_ref, QKV_COLS)):
        c_tab, s_lo, s_hi = tabs[0], tabs[1], tabs[2]
        t = proj(base, QKV_COLS)
        for s in range(QKV_COLS // V7X_LANES):
            sl = slice(s * V7X_LANES, (s + 1) * V7X_LANES)
            out_ref[0, :, sl] = _rope(t[:, sl], c_tab, s_lo, s_hi).astype(_BF16)

    v_ref[0] = proj(2 * QKV_COLS, QKV_COLS).astype(_BF16)

    tri = (lax.broadcasted_iota(jnp.int32, (CHUNK, CHUNK), 0)
           >= lax.broadcasted_iota(jnp.int32, (CHUNK, CHUNK), 1))
    for g in range(SGU_GROUPS):
        cols = slice(g * SGU_GROUP_DIM, (g + 1) * SGU_GROUP_DIM)
        w_g = jnp.where(tri, sw_ref[g], 0.0).astype(_BF16)
        b_g = sbt_ref[:, g:g + 1]
        v_g = jnp.concatenate([vv[n * CHUNK:(n + 1) * CHUNK, cols] for n in range(n_chunks)], axis=1)
        sv = jnp.dot(w_g, v_g, preferred_element_type=_F32) + b_g
        for n in range(n_chunks):
            rws = slice(n * CHUNK, (n + 1) * CHUNK)
            sgu_ref[0, rws, cols] = (
                u[rws, cols] * sv[:, n * SGU_GROUP_DIM:(n + 1) * SGU_GROUP_DIM]).astype(_BF16)


def _row_block_spec(n_rows, n_cols, n_steps, steps_per_batch):
    n_blocks = n_steps
    while n_rows % (n_blocks * V7X_BF16_SUBLANES):
        n_blocks //= 2
    repeat = n_steps // n_blocks
    return pl.BlockSpec((n_rows // n_blocks, n_cols),
                        lambda b, i: ((b * steps_per_batch + i) // repeat, 0))


def _mod_spec(B, D, which):
    return pl.BlockSpec((B, D), lambda b, i: (0, which))


def _in_proj(x, mod, w_in, rope_q, rope_k, sgu_ln_g, sgu_ln_b, sgu_w, sgu_bt, later_weights):
    B, S, D = x.shape
    P = w_in.shape[1]
    rows = PROJ_ROWS
    steps_per_batch = S // rows
    tok = lambda width: pl.BlockSpec((1, rows, width), lambda b, i: (b, i, 0))
    rope_spec = pl.BlockSpec((3, rows, V7X_LANES), lambda b, i: (0, i, 0))
    out_sd = jax.ShapeDtypeStruct((B, S, QKV_COLS), _BF16)
    cast_specs = [_row_block_spec(w.shape[0], w.shape[1], B * steps_per_batch, steps_per_batch)
                  for w in later_weights]
    outs = pl.pallas_call(
        _in_proj_kernel,
        grid=(B, steps_per_batch),
        in_specs=[
            tok(D), _mod_spec(B, D, MOD_SCALE1), _mod_spec(B, D, MOD_SHIFT1),
            _resident((D, P)), rope_spec, rope_spec,
            _resident((1, SGU_WIDTH)), _resident((1, SGU_WIDTH)),
            _resident((SGU_GROUPS, CHUNK, CHUNK)), _resident((CHUNK, SGU_GROUPS)),
        ] + cast_specs,
        out_specs=[tok(QKV_COLS), tok(QKV_COLS), tok(QKV_COLS), tok(SGU_WIDTH)] + cast_specs,
        out_shape=[out_sd, out_sd, out_sd, jax.ShapeDtypeStruct((B, S, SGU_WIDTH), _BF16)]
        + [jax.ShapeDtypeStruct(w.shape, _BF16) for w in later_weights],
        compiler_params=pltpu.CompilerParams(
            dimension_semantics=("arbitrary", "arbitrary"),
            vmem_limit_bytes=V7X_VMEM_LIMIT_BYTES),
        name="in_proj",
    )(x, mod, mod, w_in, rope_q, rope_k, sgu_ln_g, sgu_ln_b, sgu_w, sgu_bt, *later_weights)
    return outs[:4], outs[4:]


def _attn_setup(lq1_ref, lk1_ref, lq2_ref, lk2_ref, g_ref, lam_init):
    tq = ATTN_Q_ROWS
    lam = (jnp.exp(jnp.sum(lq1_ref[...] * lk1_ref[...], axis=-1, keepdims=True))
           - jnp.exp(jnp.sum(lq2_ref[...] * lk2_ref[...], axis=-1, keepdims=True))
           + lam_init)
    lane = lax.broadcasted_iota(jnp.int32, (1, V7X_LANES), 1)
    causal = (lax.broadcasted_iota(jnp.int32, (tq, tq), 1)
              <= lax.broadcasted_iota(jnp.int32, (tq, tq), 0))
    gain = g_ref[...] * (1.0 - lam_init)
    return lam, lane, lane < DIFF_HEAD_DIM, causal, gain


def _subln_store(o_ref, i, h, o, gain):
    tq = ATTN_Q_ROWS
    y = o * lax.rsqrt(jnp.mean(o * o, axis=-1, keepdims=True) + LN_EPS) * gain
    o_ref[0, i * tq:(i + 1) * tq, h * DIFF_VDIM:(h + 1) * DIFF_VDIM] = y.astype(_BF16)


def _diff_attn_fast_kernel(lq1_ref, lk1_ref, lq2_ref, lk2_ref, g_ref, q_ref, k_ref, v_ref,
                           o_ref, lrange_ref, vpad_ref, *, lam_init):
    S = q_ref.shape[1]
    n_heads = q_ref.shape[2] // DIFF_VDIM
    wide = DIFF_VDIM + V7X_LANES
    tq = ATTN_Q_ROWS
    lam, _, first, causal, gain = _attn_setup(lq1_ref, lk1_ref, lq2_ref, lk2_ref, g_ref, lam_init)
    neg = jnp.finfo(_F32).min

    for h in range(n_heads):
        vpad_ref[:, h * wide:h * wide + DIFF_VDIM] = v_ref[0, :, h * DIFF_VDIM:(h + 1) * DIFF_VDIM]
        vpad_ref[:, h * wide + DIFF_VDIM:(h + 1) * wide] = jnp.zeros((S, V7X_LANES), _BF16)

    def exp_scores(h, i):
        r0, n = i * tq, (i + 1) * tq
        cols = slice(h * DIFF_VDIM, (h + 1) * DIFF_VDIM)
        q = q_ref[0, r0:n, cols]
        zero = jnp.zeros_like(q)
        out = []
        for qc in (jnp.where(first, q, zero), jnp.where(first, zero, q)):
            s = lax.dot_general(qc, k_ref[0, 0:n, cols], _NT, preferred_element_type=_F32)
            p = jnp.exp2(jnp.where(causal, s[:, r0:n], neg))
            if i > 0:
                p = jnp.concatenate([jnp.exp2(s[:, 0:r0]), p], axis=1)
            out.append((p.astype(_BF16), jnp.sum(p, axis=-1, keepdims=True)))
        return out

    stages = [(h, i) for h in range(n_heads) for i in range(S // tq)]
    lmax = jnp.zeros((tq, 1), _F32)
    lmin = jnp.full((tq, 1), jnp.inf, _F32)
    e_next = exp_scores(*stages[0])
    for idx, (h, i) in enumerate(stages):
        (p1, l1), (p2, l2) = e_next
        if idx + 1 < len(stages):
            e_next = exp_scores(*stages[idx + 1])
        n = (i + 1) * tq
        lmax = jnp.maximum(lmax, jnp.maximum(l1, l2))
        lmin = jnp.minimum(lmin, jnp.minimum(l1, l2))
        a = p1 - (lam * l1 / l2).astype(_BF16) * p2
        kh = n // 2
        vh = slice(h * wide, (h + 1) * wide)
        o = (jnp.dot(a[:, 0:kh], vpad_ref[0:kh, vh], preferred_element_type=_F32)
             + jnp.dot(a[:, kh:n], vpad_ref[kh:n, vh], preferred_element_type=_F32))
        _subln_store(o_ref, i, h, o[:, 0:DIFF_VDIM] * (1.0 / l1), gain)
    top = jnp.max(lmax, axis=0, keepdims=True)
    bot = jnp.min(lmin, axis=0, keepdims=True)
    upper = lax.broadcasted_iota(jnp.int32, (8, V7X_LANES), 0) < 4
    lrange_ref[0, 0] = jnp.where(upper, top, bot)


def _diff_attn_exact_kernel(lq1_ref, lk1_ref, lq2_ref, lk2_ref, g_ref, q_ref, k_ref, v_ref, o_ref,
                            vaug_ref, *, lam_init):
    S = q_ref.shape[1]
    tq = ATTN_Q_ROWS
    lam, lane, first, causal, gain = _attn_setup(lq1_ref, lk1_ref, lq2_ref, lk2_ref, g_ref, lam_init)
    neg = jnp.finfo(_F32).min

    vaug_ref[:, 0:DIFF_VDIM] = v_ref[0]
    vaug_ref[:, DIFF_VDIM:] = jnp.broadcast_to(jnp.where(lane == 0, 1.0, 0.0), (S, V7X_LANES)).astype(_BF16)

    def scores(i):
        n = (i + 1) * tq
        q = q_ref[0, i * tq:n, :]
        zero = jnp.zeros_like(q)
        return [lax.dot_general(qc, k_ref[0, 0:n, :], _NT, preferred_element_type=_F32)
                for qc in (jnp.where(first, q, zero), jnp.where(first, zero, q))]

    def weighted_values(s, i):
        r0, n = i * tq, (i + 1) * tq
        s_d = jnp.where(causal, s[:, r0:n], neg)
        m = jnp.max(s_d, axis=-1, keepdims=True)
        if i > 0:
            m = jnp.maximum(m, jnp.max(s[:, 0:r0], axis=-1, keepdims=True))
            p = jnp.concatenate([jnp.exp2(s[:, 0:r0] - m), jnp.exp2(s_d - m)], axis=1)
        else:
            p = jnp.exp2(s_d - m)
        return jnp.dot(p.astype(_BF16), vaug_ref[0:n, :], preferred_element_type=_F32)

    n_blocks = S // tq
    s_next = scores(0)
    for i in range(n_blocks):
        s_cur = s_next
        if i + 1 < n_blocks:
            s_next = scores(i + 1)
        acc1 = weighted_values(s_cur[0], i)
        acc2 = weighted_values(s_cur[1], i)
        r1 = 1.0 / acc1[:, DIFF_VDIM:DIFF_VDIM + 1]
        r2 = lam / acc2[:, DIFF_VDIM:DIFF_VDIM + 1]
        _subln_store(o_ref, i, 0, acc1[:, 0:DIFF_VDIM] * r1 - acc2[:, 0:DIFF_VDIM] * r2, gain)


def _diff_attn(q, k, v, lq1, lk1, lq2, lk2, subln_g, lam_init):
    B, S, _ = q.shape
    head = pl.BlockSpec((1, S, DIFF_VDIM), lambda b, h: (b, 0, h))
    vec = _resident((1, DIFF_HEAD_DIM))
    in_specs = [vec, vec, vec, vec, _resident((1, DIFF_VDIM)), head, head, head]
    params = pltpu.CompilerParams(dimension_semantics=("arbitrary", "arbitrary"),
                                  vmem_limit_bytes=V7X_VMEM_LIMIT_BYTES)
    out_sd = jax.ShapeDtypeStruct((B, S, DIFF_WIDTH), _BF16)
    args = (lq1, lk1, lq2, lk2, subln_g, q, k, v)
    wide = DIFF_VDIM + V7X_LANES

    hps = ATTN_FAST_HEADS_PER_STEP
    heads = pl.BlockSpec((1, S, hps * DIFF_VDIM), lambda b, h: (b, 0, h))
    fast, lrange = pl.pallas_call(
        partial(_diff_attn_fast_kernel, lam_init=lam_init),
        grid=(B, N_DIFF_HEADS // hps),
        in_specs=in_specs[:5] + [heads] * 3,
        out_specs=[heads, pl.BlockSpec((1, 1, 8, V7X_LANES), lambda b, h: (b, h, 0, 0))],
        out_shape=[out_sd, jax.ShapeDtypeStruct((B, N_DIFF_HEADS // hps, 8, V7X_LANES), _F32)],
        scratch_shapes=[pltpu.VMEM((S, hps * wide), _BF16)],
        compiler_params=params,
        name="diff_attn_fast",
    )(*args)

    def exact():
        return pl.pallas_call(
            partial(_diff_attn_exact_kernel, lam_init=lam_init),
            grid=(B, N_DIFF_HEADS),
            in_specs=in_specs,
            out_specs=head,
            out_shape=out_sd,
            scratch_shapes=[pltpu.VMEM((S, wide), _BF16)],
            compiler_params=params,
            name="diff_attn_exact",
        )(*args)

    in_range = ((jnp.max(lrange[:, :, 0, 0]) <= ATTN_ROWSUM_LIMIT)
                & (jnp.min(lrange[:, :, 4, 0]) >= 1.0 / ATTN_ROWSUM_LIMIT))
    return lax.cond(in_range, lambda: fast, exact)


def _out_ffn_kernel(x_ref, attn_ref, sgu_ref, gate1_ref, shift2_ref, scale2_ref, gate2_ref,
                    wo_ref, ln1g_ref, ln1b_ref, wg_ref, wu_ref, wd_ref, ln2g_ref, ln2b_ref,
                    o_ref):
    rows = x_ref.shape[1]
    b = pl.program_id(0)
    gate1, shift2, scale2, gate2 = [r[pl.ds(b, 1), :]
                                    for r in (gate1_ref, shift2_ref, scale2_ref, gate2_ref)]
    slabs = [slice(r, r + FFN_SLAB_ROWS) for r in range(0, rows, FFN_SLAB_ROWS)]
    mix = [jnp.dot(jnp.concatenate([attn_ref[0, rs, :], sgu_ref[0, rs, :]], axis=1), wo_ref[...],
                   preferred_element_type=_F32) for rs in slabs]
    for rs, mix_s in zip(slabs, mix):
        x1 = _layernorm_rows(ALPHA * x_ref[0, rs, :] + gate1 * mix_s, ln1g_ref[...], ln1b_ref[...])
        h2 = (x1 * (1.0 + scale2) + shift2).astype(_BF16)
        g = jnp.dot(h2, wg_ref[...], preferred_element_type=_F32)
        u = jnp.dot(h2, wu_ref[...], preferred_element_type=_F32)
        a = (g * jax.nn.sigmoid(g) * u).astype(_BF16)
        ffn = jnp.dot(a, wd_ref[...], preferred_element_type=_F32)
        o_ref[0, rs, :] = _layernorm_rows(ALPHA * x1 + gate2 * ffn, ln2g_ref[...], ln2b_ref[...])


def _out_ffn(x, attn, sgu, mod, w_o, ln1_g, ln1_b, w_gate, w_up, w_down, ln2_g, ln2_b):
    B, S, D = x.shape
    F = w_gate.shape[1]
    rows = FFN_ROWS
    tok = lambda width: pl.BlockSpec((1, rows, width), lambda b, i: (b, i, 0))
    vec = _resident((1, D))
    return pl.pallas_call(
        _out_ffn_kernel,
        grid=(B, S // rows),
        in_specs=[
            tok(D), tok(DIFF_WIDTH), tok(SGU_WIDTH),
            _mod_spec(B, D, MOD_GATE1), _mod_spec(B, D, MOD_SHIFT2),
            _mod_spec(B, D, MOD_SCALE2), _mod_spec(B, D, MOD_GATE2),
            _resident((D, D)), vec, vec, _resident((D, F)), _resident((D, F)), _resident((F, D)),
            vec, vec,
        ],
        out_specs=tok(D),
        out_shape=jax.ShapeDtypeStruct((B, S, D), x.dtype),
        compiler_params=pltpu.CompilerParams(
            dimension_semantics=("arbitrary", "arbitrary"),
            vmem_limit_bytes=V7X_VMEM_LIMIT_BYTES),
        name="out_ffn",
    )(x, attn, sgu, mod, mod, mod, mod, w_o, ln1_g, ln1_b, w_gate, w_up, w_down, ln2_g, ln2_b)


def _rope_tables(seq, scale):
    half = ROT_DIM // 2
    inv_freq = np.float32(ROPE_THETA) ** (-np.arange(half, dtype=np.float32) * np.float32(2.0 / ROT_DIM))
    ang = np.arange(seq, dtype=np.float32)[:, None] * inv_freq[None, :]
    cos, sin = np.cos(ang), np.sin(ang)
    pad = np.zeros((seq, DIFF_HEAD_DIM - ROT_DIM), np.float32)
    zer = np.zeros((seq, half), np.float32)
    c_tab = np.concatenate([cos, cos, pad + 1.0], axis=-1)
    s_lo = np.concatenate([-sin, zer, pad], axis=-1)
    s_hi = np.concatenate([zer, sin, pad], axis=-1)
    tabs = np.stack([c_tab, s_lo, s_hi]).astype(np.float32) * np.float32(scale)
    return jnp.asarray(np.tile(tabs, (1, 1, V7X_LANES // DIFF_HEAD_DIM)))


def kernel(x, c, ada_w, ada_b, w_in, lambda_q1, lambda_k1, lambda_q2, lambda_k2, subln_g,
           sgu_ln_g, sgu_ln_b, sgu_w, sgu_b, w_o, ln1_g, ln1_b, w_gate, w_up, w_down,
           ln2_g, ln2_b):
    B, S, D = x.shape
    rope_q = _rope_tables(S, DIFF_HEAD_DIM ** -0.5 * math.log2(math.e))
    rope_k = _rope_tables(S, 1.0)
    for l in range(DEPTH):
        lam_init = 0.8 - 0.6 * math.exp(-0.3 * l)
        mod, w_in16 = _adaln_mod(c, ada_w[l], ada_b[l][None, :], w_in[l])
        (q, k, v, sgu), (w_o16, w_gate16, w_up16, w_down16) = _in_proj(
            x, mod, w_in16, rope_q, rope_k,
            sgu_ln_g[l][None, :], sgu_ln_b[l][None, :], sgu_w[l], sgu_b[l].T,
            (w_o[l], w_gate[l], w_up[l], w_down[l]))
        attn = _diff_attn(q, k, v, lambda_q1[l][None, :], lambda_k1[l][None, :],
                          lambda_q2[l][None, :], lambda_k2[l][None, :],
                          subln_g[l][None, :], lam_init)
        x = _out_ffn(x, attn, sgu, mod, w_o16, ln1_g[l][None, :], ln1_b[l][None, :],
                     w_gate16, w_up16, w_down16, ln2_g[l][None, :], ln2_b[l][None, :])
    return x
```

```python
import math
from functools import partial

import numpy as np
import jax
import jax.numpy as jnp
from jax import lax
from jax.experimental import pallas as pl
from jax.experimental.pallas import tpu as pltpu

D_MODEL = 1024
N_DIFF_HEADS = 4
DIFF_HEAD_DIM = 64
DIFF_VDIM = 2 * DIFF_HEAD_DIM
DIFF_WIDTH = N_DIFF_HEADS * DIFF_VDIM
SGU_WIDTH = D_MODEL - DIFF_WIDTH
SGU_GROUPS = 4
SGU_GROUP_DIM = SGU_WIDTH // SGU_GROUPS
CHUNK = 128
ROT_DIM = DIFF_HEAD_DIM // 4
ROPE_THETA = 500000.0
DEPTH = 1
ALPHA = (2 * DEPTH) ** 0.25
LN_EPS = 1e-5
QKV_COLS = DIFF_WIDTH
MOD_SHIFT1, MOD_SCALE1, MOD_GATE1, MOD_SHIFT2, MOD_SCALE2, MOD_GATE2 = range(6)

V7X_LANES = 128
V7X_SUBLANES = 8
V7X_BF16_SUBLANES = 2 * V7X_SUBLANES
V7X_VMEM_LIMIT_BYTES = 56 * 1024 * 1024

PROJ_ROWS = 1024
FFN_ROWS = 1024
FFN_SLAB_ROWS = 256
ATTN_Q_ROWS = 256
ATTN_FAST_HEADS_PER_STEP = 2
ATTN_ROWSUM_LIMIT = 2.0 ** 60
LRANGE_MIN_ROW = V7X_SUBLANES // 2
ADA_COLS = 3072

_BF16 = jnp.bfloat16
_F32 = jnp.float32
_NT = (((1,), (1,)), ((), ()))


def _resident(shape):
    zeros = (0,) * len(shape)
    return pl.BlockSpec(shape, lambda *_: zeros, pipeline_mode=pl.Buffered(1))


def _layernorm_rows(y, g, b):
    mu = jnp.mean(y, axis=-1, keepdims=True)
    d = y - mu
    var = jnp.mean(d * d, axis=-1, keepdims=True)
    return d * lax.rsqrt(var + LN_EPS) * g + b


def _adaln_kernel(c_ref, w_ref, b_ref, win_ref, o_ref, win16_ref):
    c = c_ref[...]
    c_act = (c * jax.nn.sigmoid(c)).astype(_BF16)
    acc = jnp.dot(c_act, w_ref[...].astype(_BF16), preferred_element_type=_F32)
    o_ref[...] = acc + b_ref[...]
    win16_ref[...] = win_ref[...].astype(_BF16)


def _adaln_mod(c, ada_w, ada_b, w_in):
    B, D = c.shape
    N = ada_w.shape[1]
    steps = N // ADA_COLS
    win_spec = pl.BlockSpec((w_in.shape[0] // steps, w_in.shape[1]), lambda j: (j, 0))
    return pl.pallas_call(
        _adaln_kernel,
        grid=(steps,),
        in_specs=[
            pl.BlockSpec((B, D), lambda j: (0, 0)),
            pl.BlockSpec((D, ADA_COLS), lambda j: (0, j)),
            pl.BlockSpec((1, ADA_COLS), lambda j: (0, j)),
            win_spec,
        ],
        out_specs=[pl.BlockSpec((B, ADA_COLS), lambda j: (0, j)), win_spec],
        out_shape=[jax.ShapeDtypeStruct((B, N), _F32), jax.ShapeDtypeStruct(w_in.shape, _BF16)],
        compiler_params=pltpu.CompilerParams(
            dimension_semantics=("arbitrary",), vmem_limit_bytes=V7X_VMEM_LIMIT_BYTES),
        name="adaln_mod",
    )(c, ada_w, ada_b, w_in)


def _rope(t, c_tab, s_lo, s_hi):
    up = pltpu.roll(t, V7X_LANES - ROT_DIM // 2, 1)
    dn = pltpu.roll(t, ROT_DIM // 2, 1)
    return t * c_tab + up * s_lo + dn * s_hi


def _in_proj_kernel(x_ref, scale_ref, shift_ref, w_ref, ropeq_ref, ropek_ref,
                    lng_ref, lnb_ref, sw_ref, sbt_ref, *rest):
    n_cast = (len(rest) - 4) // 2
    cast_in, (q_ref, k_ref, v_ref, sgu_ref), cast_out = (
        rest[:n_cast], rest[n_cast:n_cast + 4], rest[n_cast + 4:])
    for src, dst in zip(cast_in, cast_out):
        dst[...] = src[...].astype(_BF16)

    rows = x_ref.shape[1]
    n_chunks = rows // CHUNK
    b = pl.program_id(0)
    scale, shift = scale_ref[pl.ds(b, 1), :], shift_ref[pl.ds(b, 1), :]
    h = (x_ref[0] * (1.0 + scale) + shift).astype(_BF16)

    def proj(lo, width):
        return jnp.dot(h, w_ref[:, lo:lo + width], preferred_element_type=_F32)

    o3 = 3 * QKV_COLS
    zu = proj(o3, SGU_WIDTH)
    zv = proj(o3 + SGU_WIDTH, SGU_WIDTH)
    inv_sqrt2 = 1.0 / math.sqrt(2.0)
    u = 0.5 * zu * (1.0 + lax.erf(zu * inv_sqrt2))
    vv = 0.5 * zv * (1.0 + lax.erf(zv * inv_sqrt2))
    vv = _layernorm_rows(vv, lng_ref[...], lnb_ref[...]).astype(_BF16)

    for out_ref, tabs, base in ((q_ref, ropeq_ref, 0), (k_ref, ropek_ref, QKV_COLS)):
        c_tab, s_lo, s_hi = tabs[0], tabs[1], tabs[2]
        t = proj(base, QKV_COLS)
        for s in range(QKV_COLS // V7X_LANES):
            sl = slice(s * V7X_LANES, (s + 1) * V7X_LANES)
            out_ref[0, :, sl] = _rope(t[:, sl], c_tab, s_lo, s_hi).astype(_BF16)

    v_ref[0] = proj(2 * QKV_COLS, QKV_COLS).astype(_BF16)

    tri = (lax.broadcasted_iota(jnp.int32, (CHUNK, CHUNK), 0)
           >= lax.broadcasted_iota(jnp.int32, (CHUNK, CHUNK), 1))
    for g in range(SGU_GROUPS):
        cols = slice(g * SGU_GROUP_DIM, (g + 1) * SGU_GROUP_DIM)
        w_g = jnp.where(tri, sw_ref[g], 0.0).astype(_BF16)
        b_g = sbt_ref[:, g:g + 1]
        v_g = jnp.concatenate([vv[n * CHUNK:(n + 1) * CHUNK, cols] for n in range(n_chunks)], axis=1)
        sv = jnp.dot(w_g, v_g, preferred_element_type=_F32) + b_g
        for n in range(n_chunks):
            rws = slice(n * CHUNK, (n + 1) * CHUNK)
            sgu_ref[0, rws, cols] = (
                u[rws, cols] * sv[:, n * SGU_GROUP_DIM:(n + 1) * SGU_GROUP_DIM]).astype(_BF16)


def _row_block_spec(n_rows, n_cols, n_steps, steps_per_batch):
    n_blocks = n_steps
    while n_rows % (n_blocks * V7X_BF16_SUBLANES):
        n_blocks //= 2
    repeat = n_steps // n_blocks
    return pl.BlockSpec((n_rows // n_blocks, n_cols),
                        lambda b, i: ((b * steps_per_batch + i) // repeat, 0))


def _mod_spec(B, D, which):
    return pl.BlockSpec((B, D), lambda b, i: (0, which))


def _in_proj(x, mod, w_in, rope_q, rope_k, sgu_ln_g, sgu_ln_b, sgu_w, sgu_bt, later_weights):
    B, S, D = x.shape
    P = w_in.shape[1]
    rows = PROJ_ROWS
    steps_per_batch = S // rows
    tok = lambda width: pl.BlockSpec((1, rows, width), lambda b, i: (b, i, 0))
    rope_spec = pl.BlockSpec((3, rows, V7X_LANES), lambda b, i: (0, i, 0))
    out_sd = jax.ShapeDtypeStruct((B, S, QKV_COLS), _BF16)
    cast_specs = [_row_block_spec(w.shape[0], w.shape[1], B * steps_per_batch, steps_per_batch)
                  for w in later_weights]
    outs = pl.pallas_call(
        _in_proj_kernel,
        grid=(B, steps_per_batch),
        in_specs=[
            tok(D), _mod_spec(B, D, MOD_SCALE1), _mod_spec(B, D, MOD_SHIFT1),
            _resident((D, P)), rope_spec, rope_spec,
            _resident((1, SGU_WIDTH)), _resident((1, SGU_WIDTH)),
            _resident((SGU_GROUPS, CHUNK, CHUNK)), _resident((CHUNK, SGU_GROUPS)),
        ] + cast_specs,
        out_specs=[tok(QKV_COLS), tok(QKV_COLS), tok(QKV_COLS), tok(SGU_WIDTH)] + cast_specs,
        out_shape=[out_sd, out_sd, out_sd, jax.ShapeDtypeStruct((B, S, SGU_WIDTH), _BF16)]
        + [jax.ShapeDtypeStruct(w.shape, _BF16) for w in later_weights],
        compiler_params=pltpu.CompilerParams(
            dimension_semantics=("arbitrary", "arbitrary"),
            vmem_limit_bytes=V7X_VMEM_LIMIT_BYTES),
        name="in_proj",
    )(x, mod, mod, w_in, rope_q, rope_k, sgu_ln_g, sgu_ln_b, sgu_w, sgu_bt, *later_weights)
    return outs[:4], outs[4:]


def _attn_setup(lq1_ref, lk1_ref, lq2_ref, lk2_ref, g_ref, lam_init):
    tq = ATTN_Q_ROWS
    lam = (jnp.exp(jnp.sum(lq1_ref[...] * lk1_ref[...], axis=-1, keepdims=True))
           - jnp.exp(jnp.sum(lq2_ref[...] * lk2_ref[...], axis=-1, keepdims=True))
           + lam_init)
    lane = lax.broadcasted_iota(jnp.int32, (1, V7X_LANES), 1)
    causal = (lax.broadcasted_iota(jnp.int32, (tq, tq), 1)
              <= lax.broadcasted_iota(jnp.int32, (tq, tq), 0))
    gain = g_ref[...] * (1.0 - lam_init)
    return lam, lane, lane < DIFF_HEAD_DIM, causal, gain


def _subln_store(o_ref, i, h, o, gain):
    tq = ATTN_Q_ROWS
    y = o * lax.rsqrt(jnp.mean(o * o, axis=-1, keepdims=True) + LN_EPS) * gain
    o_ref[0, i * tq:(i + 1) * tq, h * DIFF_VDIM:(h + 1) * DIFF_VDIM] = y.astype(_BF16)


def _diff_attn_fast_kernel(lq1_ref, lk1_ref, lq2_ref, lk2_ref, g_ref, q_ref, k_ref, v_ref,
                           o_ref, lrange_ref, vpad_ref, *, lam_init):
    S = q_ref.shape[1]
    n_heads = q_ref.shape[2] // DIFF_VDIM
    wide = DIFF_VDIM + V7X_LANES
    tq = ATTN_Q_ROWS
    lam, _, first, causal, gain = _attn_setup(lq1_ref, lk1_ref, lq2_ref, lk2_ref, g_ref, lam_init)
    neg = jnp.finfo(_F32).min

    for h in range(n_heads):
        vpad_ref[:, h * wide:h * wide + DIFF_VDIM] = v_ref[0, :, h * DIFF_VDIM:(h + 1) * DIFF_VDIM]
        vpad_ref[:, h * wide + DIFF_VDIM:(h + 1) * wide] = jnp.zeros((S, V7X_LANES), _BF16)

    def exp_scores(h, i):
        r0, n = i * tq, (i + 1) * tq
        cols = slice(h * DIFF_VDIM, (h + 1) * DIFF_VDIM)
        q = q_ref[0, r0:n, cols]
        zero = jnp.zeros_like(q)
        out = []
        for qc in (jnp.where(first, q, zero), jnp.where(first, zero, q)):
            s = lax.dot_general(qc, k_ref[0, 0:n, cols], _NT, preferred_element_type=_F32)
            p = jnp.exp2(jnp.where(causal, s[:, r0:n], neg))
            if i > 0:
                p = jnp.concatenate([jnp.exp2(s[:, 0:r0]), p], axis=1)
            out.append((p.astype(_BF16), jnp.sum(p, axis=-1, keepdims=True)))
        return out

    stages = [(h, i) for i in range(S // tq) for h in range(n_heads)]
    lmax = jnp.zeros((tq, 1), _F32)
    lmin = jnp.full((tq, 1), jnp.inf, _F32)
    e_next = exp_scores(*stages[0])
    for idx, (h, i) in enumerate(stages):
        (p1, l1), (p2, l2) = e_next
        if idx + 1 < len(stages):
            e_next = exp_scores(*stages[idx + 1])
        n = (i + 1) * tq
        lmax = jnp.maximum(lmax, jnp.maximum(l1, l2))
        lmin = jnp.minimum(lmin, jnp.minimum(l1, l2))
        a = p1 - (lam * l1 / l2).astype(_BF16) * p2
        kh = n // 2
        vh = slice(h * wide, (h + 1) * wide)
        o = (jnp.dot(a[:, 0:kh], vpad_ref[0:kh, vh], preferred_element_type=_F32)
             + jnp.dot(a[:, kh:n], vpad_ref[kh:n, vh], preferred_element_type=_F32))
        _subln_store(o_ref, i, h, o[:, 0:DIFF_VDIM] * (1.0 / l1), gain)
    top = jnp.max(lmax, axis=0, keepdims=True)
    bot = jnp.min(lmin, axis=0, keepdims=True)
    upper = lax.broadcasted_iota(jnp.int32, (V7X_SUBLANES, V7X_LANES), 0) < LRANGE_MIN_ROW
    lrange_ref[0, 0] = jnp.where(upper, top, bot)


def _diff_attn_exact_kernel(lq1_ref, lk1_ref, lq2_ref, lk2_ref, g_ref, q_ref, k_ref, v_ref, o_ref,
                            vaug_ref, *, lam_init):
    S = q_ref.shape[1]
    tq = ATTN_Q_ROWS
    lam, lane, first, causal, gain = _attn_setup(lq1_ref, lk1_ref, lq2_ref, lk2_ref, g_ref, lam_init)
    neg = jnp.finfo(_F32).min

    vaug_ref[:, 0:DIFF_VDIM] = v_ref[0]
    vaug_ref[:, DIFF_VDIM:] = jnp.broadcast_to(jnp.where(lane == 0, 1.0, 0.0), (S, V7X_LANES)).astype(_BF16)

    def scores(i):
        n = (i + 1) * tq
        q = q_ref[0, i * tq:n, :]
        zero = jnp.zeros_like(q)
        return [lax.dot_general(qc, k_ref[0, 0:n, :], _NT, preferred_element_type=_F32)
                for qc in (jnp.where(first, q, zero), jnp.where(first, zero, q))]

    def weighted_values(s, i):
        r0, n = i * tq, (i + 1) * tq
        s_d = jnp.where(causal, s[:, r0:n], neg)
        m = jnp.max(s_d, axis=-1, keepdims=True)
        if i > 0:
            m = jnp.maximum(m, jnp.max(s[:, 0:r0], axis=-1, keepdims=True))
            p = jnp.concatenate([jnp.exp2(s[:, 0:r0] - m), jnp.exp2(s_d - m)], axis=1)
        else:
            p = jnp.exp2(s_d - m)
        return jnp.dot(p.astype(_BF16), vaug_ref[0:n, :], preferred_element_type=_F32)

    n_blocks = S // tq
    s_next = scores(0)
    for i in range(n_blocks):
        s_cur = s_next
        if i + 1 < n_blocks:
            s_next = scores(i + 1)
        acc1 = weighted_values(s_cur[0], i)
        acc2 = weighted_values(s_cur[1], i)
        r1 = 1.0 / acc1[:, DIFF_VDIM:DIFF_VDIM + 1]
        r2 = lam / acc2[:, DIFF_VDIM:DIFF_VDIM + 1]
        _subln_store(o_ref, i, 0, acc1[:, 0:DIFF_VDIM] * r1 - acc2[:, 0:DIFF_VDIM] * r2, gain)


def _diff_attn(q, k, v, lq1, lk1, lq2, lk2, subln_g, lam_init):
    B, S, _ = q.shape
    head = pl.BlockSpec((1, S, DIFF_VDIM), lambda b, h: (b, 0, h))
    vec = _resident((1, DIFF_HEAD_DIM))
    in_specs = [vec, vec, vec, vec, _resident((1, DIFF_VDIM)), head, head, head]
    params = pltpu.CompilerParams(dimension_semantics=("arbitrary", "arbitrary"),
                                  vmem_limit_bytes=V7X_VMEM_LIMIT_BYTES)
    out_sd = jax.ShapeDtypeStruct((B, S, DIFF_WIDTH), _BF16)
    args = (lq1, lk1, lq2, lk2, subln_g, q, k, v)
    wide = DIFF_VDIM + V7X_LANES

    hps = ATTN_FAST_HEADS_PER_STEP
    heads = pl.BlockSpec((1, S, hps * DIFF_VDIM), lambda b, h: (b, 0, h))
    fast, lrange = pl.pallas_call(
        partial(_diff_attn_fast_kernel, lam_init=lam_init),
        grid=(B, N_DIFF_HEADS // hps),
        in_specs=in_specs[:5] + [heads] * 3,
        out_specs=[heads, pl.BlockSpec((1, 1, V7X_SUBLANES, V7X_LANES), lambda b, h: (b, h, 0, 0))],
        out_shape=[out_sd, jax.ShapeDtypeStruct((B, N_DIFF_HEADS // hps, V7X_SUBLANES, V7X_LANES), _F32)],
        scratch_shapes=[pltpu.VMEM((S, hps * wide), _BF16)],
        compiler_params=params,
        name="diff_attn_fast",
    )(*args)

    def exact():
        return pl.pallas_call(
            partial(_diff_attn_exact_kernel, lam_init=lam_init),
            grid=(B, N_DIFF_HEADS),
            in_specs=in_specs,
            out_specs=head,
            out_shape=out_sd,
            scratch_shapes=[pltpu.VMEM((S, wide), _BF16)],
            compiler_params=params,
            name="diff_attn_exact",
        )(*args)

    in_range = ((jnp.max(lrange[:, :, 0, 0]) <= ATTN_ROWSUM_LIMIT)
                & (jnp.min(lrange[:, :, LRANGE_MIN_ROW, 0]) >= 1.0 / ATTN_ROWSUM_LIMIT))
    return lax.cond(in_range, lambda: fast, exact)


def _out_ffn_kernel(x_ref, attn_ref, sgu_ref, gate1_ref, shift2_ref, scale2_ref, gate2_ref,
                    wo_ref, ln1g_ref, ln1b_ref, wg_ref, wu_ref, wd_ref, ln2g_ref, ln2b_ref,
                    o_ref):
    rows = x_ref.shape[1]
    b = pl.program_id(0)
    gate1, shift2, scale2, gate2 = [r[pl.ds(b, 1), :]
                                    for r in (gate1_ref, shift2_ref, scale2_ref, gate2_ref)]
    slabs = [slice(r, r + FFN_SLAB_ROWS) for r in range(0, rows, FFN_SLAB_ROWS)]
    mix = [jnp.dot(jnp.concatenate([attn_ref[0, rs, :], sgu_ref[0, rs, :]], axis=1), wo_ref[...],
                   preferred_element_type=_F32) for rs in slabs]
    for rs, mix_s in zip(slabs, mix):
        x1 = _layernorm_rows(ALPHA * x_ref[0, rs, :] + gate1 * mix_s, ln1g_ref[...], ln1b_ref[...])
        h2 = (x1 * (1.0 + scale2) + shift2).astype(_BF16)
        g = jnp.dot(h2, wg_ref[...], preferred_element_type=_F32)
        u = jnp.dot(h2, wu_ref[...], preferred_element_type=_F32)
        a = (g * jax.nn.sigmoid(g) * u).astype(_BF16)
        ffn = jnp.dot(a, wd_ref[...], preferred_element_type=_F32)
        o_ref[0, rs, :] = _layernorm_rows(ALPHA * x1 + gate2 * ffn, ln2g_ref[...], ln2b_ref[...])


def _out_ffn(x, attn, sgu, mod, w_o, ln1_g, ln1_b, w_gate, w_up, w_down, ln2_g, ln2_b):
    B, S, D = x.shape
    F = w_gate.shape[1]
    rows = FFN_ROWS
    tok = lambda width: pl.BlockSpec((1, rows, width), lambda b, i: (b, i, 0))
    vec = _resident((1, D))
    return pl.pallas_call(
        _out_ffn_kernel,
        grid=(B, S // rows),
        in_specs=[
            tok(D), tok(DIFF_WIDTH), tok(SGU_WIDTH),
            _mod_spec(B, D, MOD_GATE1), _mod_spec(B, D, MOD_SHIFT2),
            _mod_spec(B, D, MOD_SCALE2), _mod_spec(B, D, MOD_GATE2),
            _resident((D, D)), vec, vec, _resident((D, F)), _resident((D, F)), _resident((F, D)),
            vec, vec,
        ],
        out_specs=tok(D),
        out_shape=jax.ShapeDtypeStruct((B, S, D), x.dtype),
        compiler_params=pltpu.CompilerParams(
            dimension_semantics=("arbitrary", "arbitrary"),
            vmem_limit_bytes=V7X_VMEM_LIMIT_BYTES),
        name="out_ffn",
    )(x, attn, sgu, mod, mod, mod, mod, w_o, ln1_g, ln1_b, w_gate, w_up, w_down, ln2_g, ln2_b)


def _rope_tables(seq, scale):
    half = ROT_DIM // 2
    inv_freq = np.float32(ROPE_THETA) ** (-np.arange(half, dtype=np.float32) * np.float32(2.0 / ROT_DIM))
    ang = np.arange(seq, dtype=np.float32)[:, None] * inv_freq[None, :]
    cos, sin = np.cos(ang), np.sin(ang)
    pad = np.zeros((seq, DIFF_HEAD_DIM - ROT_DIM), np.float32)
    zer = np.zeros((seq, half), np.float32)
    c_tab = np.concatenate([cos, cos, pad + 1.0], axis=-1)
    s_lo = np.concatenate([-sin, zer, pad], axis=-1)
    s_hi = np.concatenate([zer, sin, pad], axis=-1)
    tabs = np.stack([c_tab, s_lo, s_hi]).astype(np.float32) * np.float32(scale)
    return jnp.asarray(np.tile(tabs, (1, 1, V7X_LANES // DIFF_HEAD_DIM)))


def kernel(x, c, ada_w, ada_b, w_in, lambda_q1, lambda_k1, lambda_q2, lambda_k2, subln_g,
           sgu_ln_g, sgu_ln_b, sgu_w, sgu_b, w_o, ln1_g, ln1_b, w_gate, w_up, w_down,
           ln2_g, ln2_b):
    B, S, D = x.shape
    rope_q = _rope_tables(S, DIFF_HEAD_DIM ** -0.5 * math.log2(math.e))
    rope_k = _rope_tables(S, 1.0)
    for l in range(DEPTH):
        lam_init = 0.8 - 0.6 * math.exp(-0.3 * l)
        mod, w_in16 = _adaln_mod(c, ada_w[l], ada_b[l][None, :], w_in[l])
        (q, k, v, sgu), (w_o16, w_gate16, w_up16, w_down16) = _in_proj(
            x, mod, w_in16, rope_q, rope_k,
            sgu_ln_g[l][None, :], sgu_ln_b[l][None, :], sgu_w[l], sgu_b[l].T,
            (w_o[l], w_gate[l], w_up[l], w_down[l]))
        attn = _diff_attn(q, k, v, lambda_q1[l][None, :], lambda_k1[l][None, :],
                          lambda_q2[l][None, :], lambda_k2[l][None, :],
                          subln_g[l][None, :], lam_init)
        x = _out_ffn(x, attn, sgu, mod, w_o16, ln1_g[l][None, :], ln1_b[l][None, :],
                     w_gate16, w_up16, w_down16, ln2_g[l][None, :], ln2_b[l][None, :])
    return x
```

```python
import math
from functools import partial

import numpy as np
import jax
import jax.numpy as jnp
from jax import lax
from jax.experimental import pallas as pl
from jax.experimental.pallas import tpu as pltpu

D_MODEL = 1024
N_DIFF_HEADS = 4
DIFF_HEAD_DIM = 64
DIFF_VDIM = 2 * DIFF_HEAD_DIM
DIFF_WIDTH = N_DIFF_HEADS * DIFF_VDIM
SGU_WIDTH = D_MODEL - DIFF_WIDTH
SGU_GROUPS = 4
SGU_GROUP_DIM = SGU_WIDTH // SGU_GROUPS
CHUNK = 128
ROT_DIM = DIFF_HEAD_DIM // 4
ROPE_THETA = 500000.0
DEPTH = 1
ALPHA = (2 * DEPTH) ** 0.25
LN_EPS = 1e-5
QKV_COLS = DIFF_WIDTH
MOD_SHIFT1, MOD_SCALE1, MOD_GATE1, MOD_SHIFT2, MOD_SCALE2, MOD_GATE2 = range(6)

V7X_LANES = 128
V7X_SUBLANES = 8
V7X_BF16_SUBLANES = 2 * V7X_SUBLANES
V7X_VMEM_LIMIT_BYTES = 56 * 1024 * 1024

PROJ_ROWS = 1024
FFN_ROWS = 1024
FFN_SLAB_ROWS = 512
ATTN_Q_ROWS = 256
ATTN_FAST_HEADS_PER_STEP = 2
ATTN_ROWSUM_LIMIT = 2.0 ** 60
LRANGE_MIN_ROW = V7X_SUBLANES // 2
ADA_COLS = 3072

_BF16 = jnp.bfloat16
_F32 = jnp.float32
_NT = (((1,), (1,)), ((), ()))


def _resident(shape):
    zeros = (0,) * len(shape)
    return pl.BlockSpec(shape, lambda *_: zeros, pipeline_mode=pl.Buffered(1))


def _layernorm_rows(y, g, b):
    mu = jnp.mean(y, axis=-1, keepdims=True)
    d = y - mu
    var = jnp.mean(d * d, axis=-1, keepdims=True)
    return d * lax.rsqrt(var + LN_EPS) * g + b


def _adaln_kernel(c_ref, w_ref, b_ref, win_ref, o_ref, win16_ref):
    c = c_ref[...]
    c_act = (c * jax.nn.sigmoid(c)).astype(_BF16)
    acc = jnp.dot(c_act, w_ref[...].astype(_BF16), preferred_element_type=_F32)
    o_ref[...] = acc + b_ref[...]
    win16_ref[...] = win_ref[...].astype(_BF16)


def _adaln_mod(c, ada_w, ada_b, w_in):
    B, D = c.shape
    N = ada_w.shape[1]
    steps = N // ADA_COLS
    win_spec = pl.BlockSpec((w_in.shape[0] // steps, w_in.shape[1]), lambda j: (j, 0))
    return pl.pallas_call(
        _adaln_kernel,
        grid=(steps,),
        in_specs=[
            pl.BlockSpec((B, D), lambda j: (0, 0)),
            pl.BlockSpec((D, ADA_COLS), lambda j: (0, j)),
            pl.BlockSpec((1, ADA_COLS), lambda j: (0, j)),
            win_spec,
        ],
        out_specs=[pl.BlockSpec((B, ADA_COLS), lambda j: (0, j)), win_spec],
        out_shape=[jax.ShapeDtypeStruct((B, N), _F32), jax.ShapeDtypeStruct(w_in.shape, _BF16)],
        compiler_params=pltpu.CompilerParams(
            dimension_semantics=("arbitrary",), vmem_limit_bytes=V7X_VMEM_LIMIT_BYTES),
        name="adaln_mod",
    )(c, ada_w, ada_b, w_in)


def _rope(t, c_tab, s_lo, s_hi):
    up = pltpu.roll(t, V7X_LANES - ROT_DIM // 2, 1)
    dn = pltpu.roll(t, ROT_DIM // 2, 1)
    return t * c_tab + up * s_lo + dn * s_hi


def _in_proj_kernel(x_ref, scale_ref, shift_ref, w_ref, ropeq_ref, ropek_ref,
                    lng_ref, lnb_ref, sw_ref, sbt_ref, *rest):
    n_cast = (len(rest) - 4) // 2
    cast_in, (q_ref, k_ref, v_ref, sgu_ref), cast_out = (
        rest[:n_cast], rest[n_cast:n_cast + 4], rest[n_cast + 4:])
    for src, dst in zip(cast_in, cast_out):
        dst[...] = src[...].astype(_BF16)

    rows = x_ref.shape[1]
    n_chunks = rows // CHUNK
    b = pl.program_id(0)
    scale, shift = scale_ref[pl.ds(b, 1), :], shift_ref[pl.ds(b, 1), :]
    h = (x_ref[0] * (1.0 + scale) + shift).astype(_BF16)

    def proj(lo, width):
        return jnp.dot(h, w_ref[:, lo:lo + width], preferred_element_type=_F32)

    o3 = 3 * QKV_COLS
    zu = proj(o3, SGU_WIDTH)
    zv = proj(o3 + SGU_WIDTH, SGU_WIDTH)
    inv_sqrt2 = 1.0 / math.sqrt(2.0)
    u = 0.5 * zu * (1.0 + lax.erf(zu * inv_sqrt2))
    vv = 0.5 * zv * (1.0 + lax.erf(zv * inv_sqrt2))
    vv = _layernorm_rows(vv, lng_ref[...], lnb_ref[...]).astype(_BF16)

    for out_ref, tabs, base in ((q_ref, ropeq_ref, 0), (k_ref, ropek_ref, QKV_COLS)):
        c_tab, s_lo, s_hi = tabs[0], tabs[1], tabs[2]
        t = proj(base, QKV_COLS)
        for s in range(QKV_COLS // V7X_LANES):
            sl = slice(s * V7X_LANES, (s + 1) * V7X_LANES)
            out_ref[0, :, sl] = _rope(t[:, sl], c_tab, s_lo, s_hi).astype(_BF16)

    v_ref[0] = proj(2 * QKV_COLS, QKV_COLS).astype(_BF16)

    tri = (lax.broadcasted_iota(jnp.int32, (CHUNK, CHUNK), 0)
           >= lax.broadcasted_iota(jnp.int32, (CHUNK, CHUNK), 1))
    for g in range(SGU_GROUPS):
        cols = slice(g * SGU_GROUP_DIM, (g + 1) * SGU_GROUP_DIM)
        w_g = jnp.where(tri, sw_ref[g], 0.0).astype(_BF16)
        b_g = sbt_ref[:, g:g + 1]
        v_g = jnp.concatenate([vv[n * CHUNK:(n + 1) * CHUNK, cols] for n in range(n_chunks)], axis=1)
        sv = jnp.dot(w_g, v_g, preferred_element_type=_F32) + b_g
        for n in range(n_chunks):
            rws = slice(n * CHUNK, (n + 1) * CHUNK)
            sgu_ref[0, rws, cols] = (
                u[rws, cols] * sv[:, n * SGU_GROUP_DIM:(n + 1) * SGU_GROUP_DIM]).astype(_BF16)


def _row_block_spec(n_rows, n_cols, n_steps, steps_per_batch):
    n_blocks = n_steps
    while n_rows % (n_blocks * V7X_BF16_SUBLANES):
        n_blocks //= 2
    repeat = n_steps // n_blocks
    return pl.BlockSpec((n_rows // n_blocks, n_cols),
                        lambda b, i: ((b * steps_per_batch + i) // repeat, 0))


def _mod_spec(B, D, which):
    return pl.BlockSpec((B, D), lambda b, i: (0, which))


def _in_proj(x, mod, w_in, rope_q, rope_k, sgu_ln_g, sgu_ln_b, sgu_w, sgu_bt, later_weights):
    B, S, D = x.shape
    P = w_in.shape[1]
    rows = PROJ_ROWS
    steps_per_batch = S // rows
    tok = lambda width: pl.BlockSpec((1, rows, width), lambda b, i: (b, i, 0))
    rope_spec = pl.BlockSpec((3, rows, V7X_LANES), lambda b, i: (0, i, 0))
    out_sd = jax.ShapeDtypeStruct((B, S, QKV_COLS), _BF16)
    cast_specs = [_row_block_spec(w.shape[0], w.shape[1], B * steps_per_batch, steps_per_batch)
                  for w in later_weights]
    outs = pl.pallas_call(
        _in_proj_kernel,
        grid=(B, steps_per_batch),
        in_specs=[
            tok(D), _mod_spec(B, D, MOD_SCALE1), _mod_spec(B, D, MOD_SHIFT1),
            _resident((D, P)), rope_spec, rope_spec,
            _resident((1, SGU_WIDTH)), _resident((1, SGU_WIDTH)),
            _resident((SGU_GROUPS, CHUNK, CHUNK)), _resident((CHUNK, SGU_GROUPS)),
        ] + cast_specs,
        out_specs=[tok(QKV_COLS), tok(QKV_COLS), tok(QKV_COLS), tok(SGU_WIDTH)] + cast_specs,
        out_shape=[out_sd, out_sd, out_sd, jax.ShapeDtypeStruct((B, S, SGU_WIDTH), _BF16)]
        + [jax.ShapeDtypeStruct(w.shape, _BF16) for w in later_weights],
        compiler_params=pltpu.CompilerParams(
            dimension_semantics=("arbitrary", "arbitrary"),
            vmem_limit_bytes=V7X_VMEM_LIMIT_BYTES),
        name="in_proj",
    )(x, mod, mod, w_in, rope_q, rope_k, sgu_ln_g, sgu_ln_b, sgu_w, sgu_bt, *later_weights)
    return outs[:4], outs[4:]


def _attn_setup(lq1_ref, lk1_ref, lq2_ref, lk2_ref, g_ref, lam_init):
    tq = ATTN_Q_ROWS
    lam = (jnp.exp(jnp.sum(lq1_ref[...] * lk1_ref[...], axis=-1, keepdims=True))
           - jnp.exp(jnp.sum(lq2_ref[...] * lk2_ref[...], axis=-1, keepdims=True))
           + lam_init)
    lane = lax.broadcasted_iota(jnp.int32, (1, V7X_LANES), 1)
    causal = (lax.broadcasted_iota(jnp.int32, (tq, tq), 1)
              <= lax.broadcasted_iota(jnp.int32, (tq, tq), 0))
    gain = g_ref[...] * (1.0 - lam_init)
    return lam, lane, lane < DIFF_HEAD_DIM, causal, gain


def _subln_store(o_ref, i, h, o, gain):
    tq = ATTN_Q_ROWS
    y = o * lax.rsqrt(jnp.mean(o * o, axis=-1, keepdims=True) + LN_EPS) * gain
    o_ref[0, i * tq:(i + 1) * tq, h * DIFF_VDIM:(h + 1) * DIFF_VDIM] = y.astype(_BF16)


def _diff_attn_fast_kernel(lq1_ref, lk1_ref, lq2_ref, lk2_ref, g_ref, q_ref, k_ref, v_ref,
                           o_ref, lrange_ref, vpad_ref, *, lam_init):
    S = q_ref.shape[1]
    n_heads = q_ref.shape[2] // DIFF_VDIM
    wide = DIFF_VDIM + V7X_LANES
    tq = ATTN_Q_ROWS
    lam, _, first, causal, gain = _attn_setup(lq1_ref, lk1_ref, lq2_ref, lk2_ref, g_ref, lam_init)
    neg = jnp.finfo(_F32).min

    for h in range(n_heads):
        vpad_ref[:, h * wide:h * wide + DIFF_VDIM] = v_ref[0, :, h * DIFF_VDIM:(h + 1) * DIFF_VDIM]
        vpad_ref[:, h * wide + DIFF_VDIM:(h + 1) * wide] = jnp.zeros((S, V7X_LANES), _BF16)

    def exp_scores(h, i):
        r0, n = i * tq, (i + 1) * tq
        cols = slice(h * DIFF_VDIM, (h + 1) * DIFF_VDIM)
        q = q_ref[0, r0:n, cols]
        zero = jnp.zeros_like(q)
        out = []
        for qc in (jnp.where(first, q, zero), jnp.where(first, zero, q)):
            s = lax.dot_general(qc, k_ref[0, 0:n, cols], _NT, preferred_element_type=_F32)
            p = jnp.exp2(jnp.where(causal, s[:, r0:n], neg))
            if i > 0:
                p = jnp.concatenate([jnp.exp2(s[:, 0:r0]), p], axis=1)
            out.append((p.astype(_BF16), jnp.sum(p, axis=-1, keepdims=True)))
        return out

    stages = [(h, i) for h in range(n_heads) for i in range(S // tq)]
    lmax = jnp.zeros((tq, 1), _F32)
    lmin = jnp.full((tq, 1), jnp.inf, _F32)
    e_next = exp_scores(*stages[0])
    for idx, (h, i) in enumerate(stages):
        (p1, l1), (p2, l2) = e_next
        if idx + 1 < len(stages):
            e_next = exp_scores(*stages[idx + 1])
        n = (i + 1) * tq
        lmax = jnp.maximum(lmax, jnp.maximum(l1, l2))
        lmin = jnp.minimum(lmin, jnp.minimum(l1, l2))
        a = p1 - (lam * l1 / l2).astype(_BF16) * p2
        kh = n // 2
        vh = slice(h * wide, (h + 1) * wide)
        o = (jnp.dot(a[:, 0:kh], vpad_ref[0:kh, vh], preferred_element_type=_F32)
             + jnp.dot(a[:, kh:n], vpad_ref[kh:n, vh], preferred_element_type=_F32))
        _subln_store(o_ref, i, h, o[:, 0:DIFF_VDIM] * (1.0 / l1), gain)
    top = jnp.max(lmax, axis=0, keepdims=True)
    bot = jnp.min(lmin, axis=0, keepdims=True)
    upper = lax.broadcasted_iota(jnp.int32, (V7X_SUBLANES, V7X_LANES), 0) < LRANGE_MIN_ROW
    lrange_ref[0, 0] = jnp.where(upper, top, bot)


def _diff_attn_exact_kernel(lq1_ref, lk1_ref, lq2_ref, lk2_ref, g_ref, q_ref, k_ref, v_ref, o_ref,
                            vaug_ref, *, lam_init):
    S = q_ref.shape[1]
    tq = ATTN_Q_ROWS
    lam, lane, first, causal, gain = _attn_setup(lq1_ref, lk1_ref, lq2_ref, lk2_ref, g_ref, lam_init)
    neg = jnp.finfo(_F32).min

    vaug_ref[:, 0:DIFF_VDIM] = v_ref[0]
    vaug_ref[:, DIFF_VDIM:] = jnp.broadcast_to(jnp.where(lane == 0, 1.0, 0.0), (S, V7X_LANES)).astype(_BF16)

    def scores(i):
        n = (i + 1) * tq
        q = q_ref[0, i * tq:n, :]
        zero = jnp.zeros_like(q)
        return [lax.dot_general(qc, k_ref[0, 0:n, :], _NT, preferred_element_type=_F32)
                for qc in (jnp.where(first, q, zero), jnp.where(first, zero, q))]

    def weighted_values(s, i):
        r0, n = i * tq, (i + 1) * tq
        s_d = jnp.where(causal, s[:, r0:n], neg)
        m = jnp.max(s_d, axis=-1, keepdims=True)
        if i > 0:
            m = jnp.maximum(m, jnp.max(s[:, 0:r0], axis=-1, keepdims=True))
            p = jnp.concatenate([jnp.exp2(s[:, 0:r0] - m), jnp.exp2(s_d - m)], axis=1)
        else:
            p = jnp.exp2(s_d - m)
        return jnp.dot(p.astype(_BF16), vaug_ref[0:n, :], preferred_element_type=_F32)

    n_blocks = S // tq
    s_next = scores(0)
    for i in range(n_blocks):
        s_cur = s_next
        if i + 1 < n_blocks:
            s_next = scores(i + 1)
        acc1 = weighted_values(s_cur[0], i)
        acc2 = weighted_values(s_cur[1], i)
        r1 = 1.0 / acc1[:, DIFF_VDIM:DIFF_VDIM + 1]
        r2 = lam / acc2[:, DIFF_VDIM:DIFF_VDIM + 1]
        _subln_store(o_ref, i, 0, acc1[:, 0:DIFF_VDIM] * r1 - acc2[:, 0:DIFF_VDIM] * r2, gain)


def _diff_attn(q, k, v, lq1, lk1, lq2, lk2, subln_g, lam_init):
    B, S, _ = q.shape
    head = pl.BlockSpec((1, S, DIFF_VDIM), lambda b, h: (b, 0, h))
    vec = _resident((1, DIFF_HEAD_DIM))
    in_specs = [vec, vec, vec, vec, _resident((1, DIFF_VDIM)), head, head, head]
    params = pltpu.CompilerParams(dimension_semantics=("arbitrary", "arbitrary"),
                                  vmem_limit_bytes=V7X_VMEM_LIMIT_BYTES)
    out_sd = jax.ShapeDtypeStruct((B, S, DIFF_WIDTH), _BF16)
    args = (lq1, lk1, lq2, lk2, subln_g, q, k, v)
    wide = DIFF_VDIM + V7X_LANES

    hps = ATTN_FAST_HEADS_PER_STEP
    heads = pl.BlockSpec((1, S, hps * DIFF_VDIM), lambda b, h: (b, 0, h))
    fast, lrange = pl.pallas_call(
        partial(_diff_attn_fast_kernel, lam_init=lam_init),
        grid=(B, N_DIFF_HEADS // hps),
        in_specs=in_specs[:5] + [heads] * 3,
        out_specs=[heads, pl.BlockSpec((1, 1, V7X_SUBLANES, V7X_LANES), lambda b, h: (b, h, 0, 0))],
        out_shape=[out_sd, jax.ShapeDtypeStruct((B, N_DIFF_HEADS // hps, V7X_SUBLANES, V7X_LANES), _F32)],
        scratch_shapes=[pltpu.VMEM((S, hps * wide), _BF16)],
        compiler_params=params,
        name="diff_attn_fast",
    )(*args)

    def exact():
        return pl.pallas_call(
            partial(_diff_attn_exact_kernel, lam_init=lam_init),
            grid=(B, N_DIFF_HEADS),
            in_specs=in_specs,
            out_specs=head,
            out_shape=out_sd,
            scratch_shapes=[pltpu.VMEM((S, wide), _BF16)],
            compiler_params=params,
            name="diff_attn_exact",
        )(*args)

    in_range = ((jnp.max(lrange[:, :, 0, 0]) <= ATTN_ROWSUM_LIMIT)
                & (jnp.min(lrange[:, :, LRANGE_MIN_ROW, 0]) >= 1.0 / ATTN_ROWSUM_LIMIT))
    return lax.cond(in_range, lambda: fast, exact)


def _out_ffn_kernel(x_ref, attn_ref, sgu_ref, gate1_ref, shift2_ref, scale2_ref, gate2_ref,
                    wo_ref, ln1g_ref, ln1b_ref, wg_ref, wu_ref, wd_ref, ln2g_ref, ln2b_ref,
                    o_ref):
    rows = x_ref.shape[1]
    b = pl.program_id(0)
    gate1, shift2, scale2, gate2 = [r[pl.ds(b, 1), :]
                                    for r in (gate1_ref, shift2_ref, scale2_ref, gate2_ref)]
    slabs = [slice(r, r + FFN_SLAB_ROWS) for r in range(0, rows, FFN_SLAB_ROWS)]
    mix = [jnp.dot(jnp.concatenate([attn_ref[0, rs, :], sgu_ref[0, rs, :]], axis=1), wo_ref[...],
                   preferred_element_type=_F32) for rs in slabs]
    for rs, mix_s in zip(slabs, mix):
        x1 = _layernorm_rows(ALPHA * x_ref[0, rs, :] + gate1 * mix_s, ln1g_ref[...], ln1b_ref[...])
        h2 = (x1 * (1.0 + scale2) + shift2).astype(_BF16)
        g = jnp.dot(h2, wg_ref[...], preferred_element_type=_F32)
        u = jnp.dot(h2, wu_ref[...], preferred_element_type=_F32)
        a = (g * jax.nn.sigmoid(g) * u).astype(_BF16)
        ffn = jnp.dot(a, wd_ref[...], preferred_element_type=_F32)
        o_ref[0, rs, :] = _layernorm_rows(ALPHA * x1 + gate2 * ffn, ln2g_ref[...], ln2b_ref[...])


def _out_ffn(x, attn, sgu, mod, w_o, ln1_g, ln1_b, w_gate, w_up, w_down, ln2_g, ln2_b):
    B, S, D = x.shape
    F = w_gate.shape[1]
    rows = FFN_ROWS
    tok = lambda width: pl.BlockSpec((1, rows, width), lambda b, i: (b, i, 0))
    vec = _resident((1, D))
    return pl.pallas_call(
        _out_ffn_kernel,
        grid=(B, S // rows),
        in_specs=[
            tok(D), tok(DIFF_WIDTH), tok(SGU_WIDTH),
            _mod_spec(B, D, MOD_GATE1), _mod_spec(B, D, MOD_SHIFT2),
            _mod_spec(B, D, MOD_SCALE2), _mod_spec(B, D, MOD_GATE2),
            _resident((D, D)), vec, vec, _resident((D, F)), _resident((D, F)), _resident((F, D)),
            vec, vec,
        ],
        out_specs=tok(D),
        out_shape=jax.ShapeDtypeStruct((B, S, D), x.dtype),
        compiler_params=pltpu.CompilerParams(
            dimension_semantics=("arbitrary", "arbitrary"),
            vmem_limit_bytes=V7X_VMEM_LIMIT_BYTES),
        name="out_ffn",
    )(x, attn, sgu, mod, mod, mod, mod, w_o, ln1_g, ln1_b, w_gate, w_up, w_down, ln2_g, ln2_b)


def _rope_tables(seq, scale):
    half = ROT_DIM // 2
    inv_freq = np.float32(ROPE_THETA) ** (-np.arange(half, dtype=np.float32) * np.float32(2.0 / ROT_DIM))
    ang = np.arange(seq, dtype=np.float32)[:, None] * inv_freq[None, :]
    cos, sin = np.cos(ang), np.sin(ang)
    pad = np.zeros((seq, DIFF_HEAD_DIM - ROT_DIM), np.float32)
    zer = np.zeros((seq, half), np.float32)
    c_tab = np.concatenate([cos, cos, pad + 1.0], axis=-1)
    s_lo = np.concatenate([-sin, zer, pad], axis=-1)
    s_hi = np.concatenate([zer, sin, pad], axis=-1)
    tabs = np.stack([c_tab, s_lo, s_hi]).astype(np.float32) * np.float32(scale)
    return jnp.asarray(np.tile(tabs, (1, 1, V7X_LANES // DIFF_HEAD_DIM)))


def kernel(x, c, ada_w, ada_b, w_in, lambda_q1, lambda_k1, lambda_q2, lambda_k2, subln_g,
           sgu_ln_g, sgu_ln_b, sgu_w, sgu_b, w_o, ln1_g, ln1_b, w_gate, w_up, w_down,
           ln2_g, ln2_b):
    B, S, D = x.shape
    rope_q = _rope_tables(S, DIFF_HEAD_DIM ** -0.5 * math.log2(math.e))
    rope_k = _rope_tables(S, 1.0)
    for l in range(DEPTH):
        lam_init = 0.8 - 0.6 * math.exp(-0.3 * l)
        mod, w_in16 = _adaln_mod(c, ada_w[l], ada_b[l][None, :], w_in[l])
        (q, k, v, sgu), (w_o16, w_gate16, w_up16, w_down16) = _in_proj(
            x, mod, w_in16, rope_q, rope_k,
            sgu_ln_g[l][None, :], sgu_ln_b[l][None, :], sgu_w[l], sgu_b[l].T,
            (w_o[l], w_gate[l], w_up[l], w_down[l]))
        attn = _diff_attn(q, k, v, lambda_q1[l][None, :], lambda_k1[l][None, :],
                          lambda_q2[l][None, :], lambda_k2[l][None, :],
                          subln_g[l][None, :], lam_init)
        x = _out_ffn(x, attn, sgu, mod, w_o16, ln1_g[l][None, :], ln1_b[l][None, :],
                     w_gate16, w_up16, w_down16, ln2_g[l][None, :], ln2_b[l][None, :])
    return x
```

```python
import math
from functools import partial

import numpy as np
import jax
import jax.numpy as jnp
from jax import lax
from jax.experimental import pallas as pl
from jax.experimental.pallas import tpu as pltpu

D_MODEL = 1024
N_DIFF_HEADS = 4
DIFF_HEAD_DIM = 64
DIFF_VDIM = 2 * DIFF_HEAD_DIM
DIFF_WIDTH = N_DIFF_HEADS * DIFF_VDIM
SGU_WIDTH = D_MODEL - DIFF_WIDTH
SGU_GROUPS = 4
SGU_GROUP_DIM = SGU_WIDTH // SGU_GROUPS
CHUNK = 128
ROT_DIM = DIFF_HEAD_DIM // 4
ROPE_THETA = 500000.0
DEPTH = 1
ALPHA = (2 * DEPTH) ** 0.25
LN_EPS = 1e-5
QKV_COLS = DIFF_WIDTH
MOD_SHIFT1, MOD_SCALE1, MOD_GATE1, MOD_SHIFT2, MOD_SCALE2, MOD_GATE2 = range(6)

V7X_LANES = 128
V7X_SUBLANES = 8
V7X_BF16_SUBLANES = 2 * V7X_SUBLANES
V7X_VMEM_LIMIT_BYTES = 56 * 1024 * 1024

PROJ_ROWS = 1024
FFN_ROWS = 1024
FFN_SLAB_ROWS = 256
ATTN_Q_ROWS = 256
ATTN_FAST_HEADS_PER_STEP = 2
ATTN_ROWSUM_LIMIT = 2.0 ** 60
LRANGE_MIN_ROW = V7X_SUBLANES // 2
ADA_COLS = 3072

_BF16 = jnp.bfloat16
_F32 = jnp.float32
_NT = (((1,), (1,)), ((), ()))


def _resident(shape):
    zeros = (0,) * len(shape)
    return pl.BlockSpec(shape, lambda *_: zeros, pipeline_mode=pl.Buffered(1))


def _layernorm_rows(y, g, b):
    mu = jnp.mean(y, axis=-1, keepdims=True)
    d = y - mu
    var = jnp.mean(d * d, axis=-1, keepdims=True)
    return d * lax.rsqrt(var + LN_EPS) * g + b


def _adaln_kernel(c_ref, w_ref, b_ref, win_ref, o_ref, win16_ref):
    c = c_ref[...]
    c_act = (c * jax.nn.sigmoid(c)).astype(_BF16)
    acc = jnp.dot(c_act, w_ref[...].astype(_BF16), preferred_element_type=_F32)
    o_ref[...] = acc + b_ref[...]
    win16_ref[...] = win_ref[...].astype(_BF16)


def _adaln_mod(c, ada_w, ada_b, w_in):
    B, D = c.shape
    N = ada_w.shape[1]
    steps = N // ADA_COLS
    win_spec = pl.BlockSpec((w_in.shape[0] // steps, w_in.shape[1]), lambda j: (j, 0))
    return pl.pallas_call(
        _adaln_kernel,
        grid=(steps,),
        in_specs=[
            pl.BlockSpec((B, D), lambda j: (0, 0)),
            pl.BlockSpec((D, ADA_COLS), lambda j: (0, j)),
            pl.BlockSpec((1, ADA_COLS), lambda j: (0, j)),
            win_spec,
        ],
        out_specs=[pl.BlockSpec((B, ADA_COLS), lambda j: (0, j)), win_spec],
        out_shape=[jax.ShapeDtypeStruct((B, N), _F32), jax.ShapeDtypeStruct(w_in.shape, _BF16)],
        compiler_params=pltpu.CompilerParams(
            dimension_semantics=("arbitrary",), vmem_limit_bytes=V7X_VMEM_LIMIT_BYTES),
        name="adaln_mod",
    )(c, ada_w, ada_b, w_in)


def _rope(t, c_tab, s_lo, s_hi):
    up = pltpu.roll(t, V7X_LANES - ROT_DIM // 2, 1)
    dn = pltpu.roll(t, ROT_DIM // 2, 1)
    return t * c_tab + up * s_lo + dn * s_hi


def _in_proj_kernel(x_ref, scale_ref, shift_ref, w_ref, ropeq_ref, ropek_ref,
                    lng_ref, lnb_ref, sw_ref, sbt_ref, *rest):
    n_cast = (len(rest) - 4) // 2
    cast_in, (q_ref, k_ref, v_ref, sgu_ref), cast_out = (
        rest[:n_cast], rest[n_cast:n_cast + 4], rest[n_cast + 4:])
    for src, dst in zip(cast_in, cast_out):
        dst[...] = src[...].astype(_BF16)

    rows = x_ref.shape[1]
    n_chunks = rows // CHUNK
    b = pl.program_id(0)
    scale, shift = scale_ref[pl.ds(b, 1), :], shift_ref[pl.ds(b, 1), :]
    h = (x_ref[0] * (1.0 + scale) + shift).astype(_BF16)

    def proj(lo, width):
        return jnp.dot(h, w_ref[:, lo:lo + width], preferred_element_type=_F32)

    o3 = 3 * QKV_COLS
    zu = proj(o3, SGU_WIDTH)
    zv = proj(o3 + SGU_WIDTH, SGU_WIDTH)
    inv_sqrt2 = 1.0 / math.sqrt(2.0)
    u = 0.5 * zu * (1.0 + lax.erf(zu * inv_sqrt2))
    vv = 0.5 * zv * (1.0 + lax.erf(zv * inv_sqrt2))
    vv = _layernorm_rows(vv, lng_ref[...], lnb_ref[...]).astype(_BF16)

    for out_ref, tabs, base in ((q_ref, ropeq_ref, 0), (k_ref, ropek_ref, QKV_COLS)):
        c_tab, s_lo, s_hi = tabs[0], tabs[1], tabs[2]
        t = proj(base, QKV_COLS)
        for s in range(QKV_COLS // V7X_LANES):
            sl = slice(s * V7X_LANES, (s + 1) * V7X_LANES)
            out_ref[0, :, sl] = _rope(t[:, sl], c_tab, s_lo, s_hi).astype(_BF16)

    v_ref[0] = proj(2 * QKV_COLS, QKV_COLS).astype(_BF16)

    tri = (lax.broadcasted_iota(jnp.int32, (CHUNK, CHUNK), 0)
           >= lax.broadcasted_iota(jnp.int32, (CHUNK, CHUNK), 1))
    for g in range(SGU_GROUPS):
        cols = slice(g * SGU_GROUP_DIM, (g + 1) * SGU_GROUP_DIM)
        w_g = jnp.where(tri, sw_ref[g], 0.0).astype(_BF16)
        b_g = sbt_ref[:, g:g + 1]
        v_g = jnp.concatenate([vv[n * CHUNK:(n + 1) * CHUNK, cols] for n in range(n_chunks)], axis=1)
        sv = jnp.dot(w_g, v_g, preferred_element_type=_F32) + b_g
        for n in range(n_chunks):
            rws = slice(n * CHUNK, (n + 1) * CHUNK)
            sgu_ref[0, rws, cols] = (
                u[rws, cols] * sv[:, n * SGU_GROUP_DIM:(n + 1) * SGU_GROUP_DIM]).astype(_BF16)


def _row_block_spec(n_rows, n_cols, n_steps, steps_per_batch):
    n_blocks = n_steps
    while n_rows % (n_blocks * V7X_BF16_SUBLANES):
        n_blocks //= 2
    repeat = n_steps // n_blocks
    return pl.BlockSpec((n_rows // n_blocks, n_cols),
                        lambda b, i: ((b * steps_per_batch + i) // repeat, 0))


def _mod_spec(B, D, which):
    return pl.BlockSpec((B, D), lambda b, i: (0, which))


def _in_proj(x, mod, w_in, rope_q, rope_k, sgu_ln_g, sgu_ln_b, sgu_w, sgu_bt, later_weights):
    B, S, D = x.shape
    P = w_in.shape[1]
    rows = PROJ_ROWS
    steps_per_batch = S // rows
    tok = lambda width: pl.BlockSpec((1, rows, width), lambda b, i: (b, i, 0))
    rope_spec = pl.BlockSpec((3, rows, V7X_LANES), lambda b, i: (0, i, 0))
    out_sd = jax.ShapeDtypeStruct((B, S, QKV_COLS), _BF16)
    cast_specs = [_row_block_spec(w.shape[0], w.shape[1], B * steps_per_batch, steps_per_batch)
                  for w in later_weights]
    outs = pl.pallas_call(
        _in_proj_kernel,
        grid=(B, steps_per_batch),
        in_specs=[
            tok(D), _mod_spec(B, D, MOD_SCALE1), _mod_spec(B, D, MOD_SHIFT1),
            _resident((D, P)), rope_spec, rope_spec,
            _resident((1, SGU_WIDTH)), _resident((1, SGU_WIDTH)),
            _resident((SGU_GROUPS, CHUNK, CHUNK)), _resident((CHUNK, SGU_GROUPS)),
        ] + cast_specs,
        out_specs=[tok(QKV_COLS), tok(QKV_COLS), tok(QKV_COLS), tok(SGU_WIDTH)] + cast_specs,
        out_shape=[out_sd, out_sd, out_sd, jax.ShapeDtypeStruct((B, S, SGU_WIDTH), _BF16)]
        + [jax.ShapeDtypeStruct(w.shape, _BF16) for w in later_weights],
        compiler_params=pltpu.CompilerParams(
            dimension_semantics=("arbitrary", "arbitrary"),
            vmem_limit_bytes=V7X_VMEM_LIMIT_BYTES),
        name="in_proj",
    )(x, mod, mod, w_in, rope_q, rope_k, sgu_ln_g, sgu_ln_b, sgu_w, sgu_bt, *later_weights)
    return outs[:4], outs[4:]


def _attn_setup(lq1_ref, lk1_ref, lq2_ref, lk2_ref, g_ref, lam_init):
    tq = ATTN_Q_ROWS
    lam = (jnp.exp(jnp.sum(lq1_ref[...] * lk1_ref[...], axis=-1, keepdims=True))
           - jnp.exp(jnp.sum(lq2_ref[...] * lk2_ref[...], axis=-1, keepdims=True))
           + lam_init)
    lane = lax.broadcasted_iota(jnp.int32, (1, V7X_LANES), 1)
    causal = (lax.broadcasted_iota(jnp.int32, (tq, tq), 1)
              <= lax.broadcasted_iota(jnp.int32, (tq, tq), 0))
    gain = g_ref[...] * (1.0 - lam_init)
    return lam, lane, lane < DIFF_HEAD_DIM, causal, gain


def _subln_store(o_ref, i, h, o, gain):
    tq = ATTN_Q_ROWS
    y = o * lax.rsqrt(jnp.mean(o * o, axis=-1, keepdims=True) + LN_EPS) * gain
    o_ref[0, i * tq:(i + 1) * tq, h * DIFF_VDIM:(h + 1) * DIFF_VDIM] = y.astype(_BF16)


def _diff_attn_fast_kernel(lq1_ref, lk1_ref, lq2_ref, lk2_ref, g_ref, q_ref, k_ref, v_ref,
                           o_ref, lrange_ref, *, lam_init):
    S = q_ref.shape[1]
    n_heads = q_ref.shape[2] // DIFF_VDIM
    tq = ATTN_Q_ROWS
    lam, _, first, causal, gain = _attn_setup(lq1_ref, lk1_ref, lq2_ref, lk2_ref, g_ref, lam_init)
    neg = jnp.finfo(_F32).min

    def exp_scores(h, i):
        r0, n = i * tq, (i + 1) * tq
        cols = slice(h * DIFF_VDIM, (h + 1) * DIFF_VDIM)
        q = q_ref[0, r0:n, cols]
        zero = jnp.zeros_like(q)
        out = []
        for qc in (jnp.where(first, q, zero), jnp.where(first, zero, q)):
            s = lax.dot_general(qc, k_ref[0, 0:n, cols], _NT, preferred_element_type=_F32)
            p = jnp.exp2(jnp.where(causal, s[:, r0:n], neg))
            if i > 0:
                p = jnp.concatenate([jnp.exp2(s[:, 0:r0]), p], axis=1)
            out.append((p.astype(_BF16), jnp.sum(p, axis=-1, keepdims=True)))
        return out

    stages = [(h, i) for h in range(n_heads) for i in range(S // tq)]
    lmax = jnp.zeros((tq, 1), _F32)
    lmin = jnp.full((tq, 1), jnp.inf, _F32)
    e_next = exp_scores(*stages[0])
    for idx, (h, i) in enumerate(stages):
        (p1, l1), (p2, l2) = e_next
        if idx + 1 < len(stages):
            e_next = exp_scores(*stages[idx + 1])
        n = (i + 1) * tq
        lmax = jnp.maximum(lmax, jnp.maximum(l1, l2))
        lmin = jnp.minimum(lmin, jnp.minimum(l1, l2))
        a = p1 - (lam * l1 / l2).astype(_BF16) * p2
        kh = n // 2
        vh = slice(h * DIFF_VDIM, (h + 1) * DIFF_VDIM)
        o = (jnp.dot(a[:, 0:kh], v_ref[0, 0:kh, vh], preferred_element_type=_F32)
             + jnp.dot(a[:, kh:n], v_ref[0, kh:n, vh], preferred_element_type=_F32))
        _subln_store(o_ref, i, h, o[:, 0:DIFF_VDIM] * (1.0 / l1), gain)
    top = jnp.max(lmax, axis=0, keepdims=True)
    bot = jnp.min(lmin, axis=0, keepdims=True)
    upper = lax.broadcasted_iota(jnp.int32, (V7X_SUBLANES, V7X_LANES), 0) < LRANGE_MIN_ROW
    lrange_ref[0, 0] = jnp.where(upper, top, bot)


def _diff_attn_exact_kernel(lq1_ref, lk1_ref, lq2_ref, lk2_ref, g_ref, q_ref, k_ref, v_ref, o_ref,
                            vaug_ref, *, lam_init):
    S = q_ref.shape[1]
    tq = ATTN_Q_ROWS
    lam, lane, first, causal, gain = _attn_setup(lq1_ref, lk1_ref, lq2_ref, lk2_ref, g_ref, lam_init)
    neg = jnp.finfo(_F32).min

    vaug_ref[:, 0:DIFF_VDIM] = v_ref[0]
    vaug_ref[:, DIFF_VDIM:] = jnp.broadcast_to(jnp.where(lane == 0, 1.0, 0.0), (S, V7X_LANES)).astype(_BF16)

    def scores(i):
        n = (i + 1) * tq
        q = q_ref[0, i * tq:n, :]
        zero = jnp.zeros_like(q)
        return [lax.dot_general(qc, k_ref[0, 0:n, :], _NT, preferred_element_type=_F32)
                for qc in (jnp.where(first, q, zero), jnp.where(first, zero, q))]

    def weighted_values(s, i):
        r0, n = i * tq, (i + 1) * tq
        s_d = jnp.where(causal, s[:, r0:n], neg)
        m = jnp.max(s_d, axis=-1, keepdims=True)
        if i > 0:
            m = jnp.maximum(m, jnp.max(s[:, 0:r0], axis=-1, keepdims=True))
            p = jnp.concatenate([jnp.exp2(s[:, 0:r0] - m), jnp.exp2(s_d - m)], axis=1)
        else:
            p = jnp.exp2(s_d - m)
        return jnp.dot(p.astype(_BF16), vaug_ref[0:n, :], preferred_element_type=_F32)

    n_blocks = S // tq
    s_next = scores(0)
    for i in range(n_blocks):
        s_cur = s_next
        if i + 1 < n_blocks:
            s_next = scores(i + 1)
        acc1 = weighted_values(s_cur[0], i)
        acc2 = weighted_values(s_cur[1], i)
        r1 = 1.0 / acc1[:, DIFF_VDIM:DIFF_VDIM + 1]
        r2 = lam / acc2[:, DIFF_VDIM:DIFF_VDIM + 1]
        _subln_store(o_ref, i, 0, acc1[:, 0:DIFF_VDIM] * r1 - acc2[:, 0:DIFF_VDIM] * r2, gain)


def _diff_attn(q, k, v, lq1, lk1, lq2, lk2, subln_g, lam_init):
    B, S, _ = q.shape
    head = pl.BlockSpec((1, S, DIFF_VDIM), lambda b, h: (b, 0, h))
    vec = _resident((1, DIFF_HEAD_DIM))
    in_specs = [vec, vec, vec, vec, _resident((1, DIFF_VDIM)), head, head, head]
    params = pltpu.CompilerParams(dimension_semantics=("arbitrary", "arbitrary"),
                                  vmem_limit_bytes=V7X_VMEM_LIMIT_BYTES)
    out_sd = jax.ShapeDtypeStruct((B, S, DIFF_WIDTH), _BF16)
    args = (lq1, lk1, lq2, lk2, subln_g, q, k, v)
    wide = DIFF_VDIM + V7X_LANES

    hps = ATTN_FAST_HEADS_PER_STEP
    heads = pl.BlockSpec((1, S, hps * DIFF_VDIM), lambda b, h: (b, 0, h))
    fast, lrange = pl.pallas_call(
        partial(_diff_attn_fast_kernel, lam_init=lam_init),
        grid=(B, N_DIFF_HEADS // hps),
        in_specs=in_specs[:5] + [heads] * 3,
        out_specs=[heads, pl.BlockSpec((1, 1, V7X_SUBLANES, V7X_LANES), lambda b, h: (b, h, 0, 0))],
        out_shape=[out_sd, jax.ShapeDtypeStruct((B, N_DIFF_HEADS // hps, V7X_SUBLANES, V7X_LANES), _F32)],
        compiler_params=params,
        name="diff_attn_fast",
    )(*args)

    def exact():
        return pl.pallas_call(
            partial(_diff_attn_exact_kernel, lam_init=lam_init),
            grid=(B, N_DIFF_HEADS),
            in_specs=in_specs,
            out_specs=head,
            out_shape=out_sd,
            scratch_shapes=[pltpu.VMEM((S, wide), _BF16)],
            compiler_params=params,
            name="diff_attn_exact",
        )(*args)

    in_range = ((jnp.max(lrange[:, :, 0, 0]) <= ATTN_ROWSUM_LIMIT)
                & (jnp.min(lrange[:, :, LRANGE_MIN_ROW, 0]) >= 1.0 / ATTN_ROWSUM_LIMIT))
    return lax.cond(in_range, lambda: fast, exact)


def _out_ffn_kernel(x_ref, attn_ref, sgu_ref, gate1_ref, shift2_ref, scale2_ref, gate2_ref,
                    wo_ref, ln1g_ref, ln1b_ref, wg_ref, wu_ref, wd_ref, ln2g_ref, ln2b_ref,
                    o_ref):
    rows = x_ref.shape[1]
    b = pl.program_id(0)
    gate1, shift2, scale2, gate2 = [r[pl.ds(b, 1), :]
                                    for r in (gate1_ref, shift2_ref, scale2_ref, gate2_ref)]
    slabs = [slice(r, r + FFN_SLAB_ROWS) for r in range(0, rows, FFN_SLAB_ROWS)]
    mix = [jnp.dot(jnp.concatenate([attn_ref[0, rs, :], sgu_ref[0, rs, :]], axis=1), wo_ref[...],
                   preferred_element_type=_F32) for rs in slabs]
    for rs, mix_s in zip(slabs, mix):
        x1 = _layernorm_rows(ALPHA * x_ref[0, rs, :] + gate1 * mix_s, ln1g_ref[...], ln1b_ref[...])
        h2 = (x1 * (1.0 + scale2) + shift2).astype(_BF16)
        g = jnp.dot(h2, wg_ref[...], preferred_element_type=_F32)
        u = jnp.dot(h2, wu_ref[...], preferred_element_type=_F32)
        a = (g * jax.nn.sigmoid(g) * u).astype(_BF16)
        ffn = jnp.dot(a, wd_ref[...], preferred_element_type=_F32)
        o_ref[0, rs, :] = _layernorm_rows(ALPHA * x1 + gate2 * ffn, ln2g_ref[...], ln2b_ref[...])


def _out_ffn(x, attn, sgu, mod, w_o, ln1_g, ln1_b, w_gate, w_up, w_down, ln2_g, ln2_b):
    B, S, D = x.shape
    F = w_gate.shape[1]
    rows = FFN_ROWS
    tok = lambda width: pl.BlockSpec((1, rows, width), lambda b, i: (b, i, 0))
    vec = _resident((1, D))
    return pl.pallas_call(
        _out_ffn_kernel,
        grid=(B, S // rows),
        in_specs=[
            tok(D), tok(DIFF_WIDTH), tok(SGU_WIDTH),
            _mod_spec(B, D, MOD_GATE1), _mod_spec(B, D, MOD_SHIFT2),
            _mod_spec(B, D, MOD_SCALE2), _mod_spec(B, D, MOD_GATE2),
            _resident((D, D)), vec, vec, _resident((D, F)), _resident((D, F)), _resident((F, D)),
            vec, vec,
        ],
        out_specs=tok(D),
        out_shape=jax.ShapeDtypeStruct((B, S, D), x.dtype),
        compiler_params=pltpu.CompilerParams(
            dimension_semantics=("arbitrary", "arbitrary"),
            vmem_limit_bytes=V7X_VMEM_LIMIT_BYTES),
        name="out_ffn",
    )(x, attn, sgu, mod, mod, mod, mod, w_o, ln1_g, ln1_b, w_gate, w_up, w_down, ln2_g, ln2_b)


def _rope_tables(seq, scale):
    half = ROT_DIM // 2
    inv_freq = np.float32(ROPE_THETA) ** (-np.arange(half, dtype=np.float32) * np.float32(2.0 / ROT_DIM))
    ang = np.arange(seq, dtype=np.float32)[:, None] * inv_freq[None, :]
    cos, sin = np.cos(ang), np.sin(ang)
    pad = np.zeros((seq, DIFF_HEAD_DIM - ROT_DIM), np.float32)
    zer = np.zeros((seq, half), np.float32)
    c_tab = np.concatenate([cos, cos, pad + 1.0], axis=-1)
    s_lo = np.concatenate([-sin, zer, pad], axis=-1)
    s_hi = np.concatenate([zer, sin, pad], axis=-1)
    tabs = np.stack([c_tab, s_lo, s_hi]).astype(np.float32) * np.float32(scale)
    return jnp.asarray(np.tile(tabs, (1, 1, V7X_LANES // DIFF_HEAD_DIM)))


def kernel(x, c, ada_w, ada_b, w_in, lambda_q1, lambda_k1, lambda_q2, lambda_k2, subln_g,
           sgu_ln_g, sgu_ln_b, sgu_w, sgu_b, w_o, ln1_g, ln1_b, w_gate, w_up, w_down,
           ln2_g, ln2_b):
    B, S, D = x.shape
    rope_q = _rope_tables(S, DIFF_HEAD_DIM ** -0.5 * math.log2(math.e))
    rope_k = _rope_tables(S, 1.0)
    for l in range(DEPTH):
        lam_init = 0.8 - 0.6 * math.exp(-0.3 * l)
        mod, w_in16 = _adaln_mod(c, ada_w[l], ada_b[l][None, :], w_in[l])
        (q, k, v, sgu), (w_o16, w_gate16, w_up16, w_down16) = _in_proj(
            x, mod, w_in16, rope_q, rope_k,
            sgu_ln_g[l][None, :], sgu_ln_b[l][None, :], sgu_w[l], sgu_b[l].T,
            (w_o[l], w_gate[l], w_up[l], w_down[l]))
        attn = _diff_attn(q, k, v, lambda_q1[l][None, :], lambda_k1[l][None, :],
                          lambda_q2[l][None, :], lambda_k2[l][None, :],
                          subln_g[l][None, :], lam_init)
        x = _out_ffn(x, attn, sgu, mod, w_o16, ln1_g[l][None, :], ln1_b[l][None, :],
                     w_gate16, w_up16, w_down16, ln2_g[l][None, :], ln2_b[l][None, :])
    return x
```

```python
import math
from functools import partial

import numpy as np
import jax
import jax.numpy as jnp
from jax import lax
from jax.experimental import pallas as pl
from jax.experimental.pallas import tpu as pltpu

D_MODEL = 1024
N_DIFF_HEADS = 4
DIFF_HEAD_DIM = 64
DIFF_VDIM = 2 * DIFF_HEAD_DIM
DIFF_WIDTH = N_DIFF_HEADS * DIFF_VDIM
SGU_WIDTH = D_MODEL - DIFF_WIDTH
SGU_GROUPS = 4
SGU_GROUP_DIM = SGU_WIDTH // SGU_GROUPS
CHUNK = 128
ROT_DIM = DIFF_HEAD_DIM // 4
ROPE_THETA = 500000.0
DEPTH = 1
ALPHA = (2 * DEPTH) ** 0.25
LN_EPS = 1e-5
QKV_COLS = DIFF_WIDTH
MOD_SHIFT1, MOD_SCALE1, MOD_GATE1, MOD_SHIFT2, MOD_SCALE2, MOD_GATE2 = range(6)

V7X_LANES = 128
V7X_SUBLANES = 8
V7X_BF16_SUBLANES = 2 * V7X_SUBLANES
V7X_VMEM_LIMIT_BYTES = 56 * 1024 * 1024

PROJ_ROWS = 1024
FFN_ROWS = 1024
FFN_SLAB_ROWS = 256
ATTN_Q_ROWS = 256
ATTN_FAST_HEADS_PER_STEP = 2
ATTN_ROWSUM_LIMIT = 2.0 ** 60
LRANGE_MIN_ROW = V7X_SUBLANES // 2
ADA_COLS = 3072

_BF16 = jnp.bfloat16
_F32 = jnp.float32
_NT = (((1,), (1,)), ((), ()))


def _resident(shape):
    zeros = (0,) * len(shape)
    return pl.BlockSpec(shape, lambda *_: zeros, pipeline_mode=pl.Buffered(1))


def _layernorm_rows(y, g, b):
    mu = jnp.mean(y, axis=-1, keepdims=True)
    d = y - mu
    var = jnp.mean(d * d, axis=-1, keepdims=True)
    return d * lax.rsqrt(var + LN_EPS) * g + b


def _adaln_kernel(c_ref, w_ref, b_ref, win_ref, o_ref, win16_ref):
    c = c_ref[...]
    c_act = (c * jax.nn.sigmoid(c)).astype(_BF16)
    acc = jnp.dot(c_act, w_ref[...].astype(_BF16), preferred_element_type=_F32)
    o_ref[...] = acc + b_ref[...]
    win16_ref[...] = win_ref[...].astype(_BF16)


def _adaln_mod(c, ada_w, ada_b, w_in):
    B, D = c.shape
    N = ada_w.shape[1]
    steps = N // ADA_COLS
    win_spec = pl.BlockSpec((w_in.shape[0] // steps, w_in.shape[1]), lambda j: (j, 0))
    return pl.pallas_call(
        _adaln_kernel,
        grid=(steps,),
        in_specs=[
            pl.BlockSpec((B, D), lambda j: (0, 0)),
            pl.BlockSpec((D, ADA_COLS), lambda j: (0, j)),
            pl.BlockSpec((1, ADA_COLS), lambda j: (0, j)),
            win_spec,
        ],
        out_specs=[pl.BlockSpec((B, ADA_COLS), lambda j: (0, j)), win_spec],
        out_shape=[jax.ShapeDtypeStruct((B, N), _F32), jax.ShapeDtypeStruct(w_in.shape, _BF16)],
        compiler_params=pltpu.CompilerParams(
            dimension_semantics=("arbitrary",), vmem_limit_bytes=V7X_VMEM_LIMIT_BYTES),
        name="adaln_mod",
    )(c, ada_w, ada_b, w_in)


def _rope(t, c_tab, s_lo, s_hi):
    up = pltpu.roll(t, V7X_LANES - ROT_DIM // 2, 1)
    dn = pltpu.roll(t, ROT_DIM // 2, 1)
    return t * c_tab + up * s_lo + dn * s_hi


def _in_proj_kernel(x_ref, scale_ref, shift_ref, w_ref, ropeq_ref, ropek_ref,
                    lng_ref, lnb_ref, sw_ref, sbt_ref, *rest):
    n_cast = (len(rest) - 4) // 2
    cast_in, (q_ref, k_ref, v_ref, sgu_ref), cast_out = (
        rest[:n_cast], rest[n_cast:n_cast + 4], rest[n_cast + 4:])
    for src, dst in zip(cast_in, cast_out):
        dst[...] = src[...].astype(_BF16)

    rows = x_ref.shape[1]
    n_chunks = rows // CHUNK
    b = pl.program_id(0)
    scale, shift = scale_ref[pl.ds(b, 1), :], shift_ref[pl.ds(b, 1), :]
    h = (x_ref[0] * (1.0 + scale) + shift).astype(_BF16)

    def proj(lo, width):
        return jnp.dot(h, w_ref[:, lo:lo + width], preferred_element_type=_F32)

    o3 = 3 * QKV_COLS
    zu = proj(o3, SGU_WIDTH)
    zv = proj(o3 + SGU_WIDTH, SGU_WIDTH)
    inv_sqrt2 = 1.0 / math.sqrt(2.0)
    u = 0.5 * zu * (1.0 + lax.erf(zu * inv_sqrt2))
    vv = 0.5 * zv * (1.0 + lax.erf(zv * inv_sqrt2))
    vv = _layernorm_rows(vv, lng_ref[...], lnb_ref[...]).astype(_BF16)

    for out_ref, tabs, base in ((q_ref, ropeq_ref, 0), (k_ref, ropek_ref, QKV_COLS)):
        c_tab, s_lo, s_hi = tabs[0], tabs[1], tabs[2]
        t = proj(base, QKV_COLS)
        for s in range(QKV_COLS // V7X_LANES):
            sl = slice(s * V7X_LANES, (s + 1) * V7X_LANES)
            out_ref[0, :, sl] = _rope(t[:, sl], c_tab, s_lo, s_hi).astype(_BF16)

    v_ref[0] = proj(2 * QKV_COLS, QKV_COLS).astype(_BF16)

    tri = (lax.broadcasted_iota(jnp.int32, (CHUNK, CHUNK), 0)
           >= lax.broadcasted_iota(jnp.int32, (CHUNK, CHUNK), 1))
    for g in range(SGU_GROUPS):
        cols = slice(g * SGU_GROUP_DIM, (g + 1) * SGU_GROUP_DIM)
        w_g = jnp.where(tri, sw_ref[g], 0.0).astype(_BF16)
        b_g = sbt_ref[:, g:g + 1]
        v_g = jnp.concatenate([vv[n * CHUNK:(n + 1) * CHUNK, cols] for n in range(n_chunks)], axis=1)
        sv = jnp.dot(w_g, v_g, preferred_element_type=_F32) + b_g
        for n in range(n_chunks):
            rws = slice(n * CHUNK, (n + 1) * CHUNK)
            sgu_ref[0, rws, cols] = (
                u[rws, cols] * sv[:, n * SGU_GROUP_DIM:(n + 1) * SGU_GROUP_DIM]).astype(_BF16)


def _row_block_spec(n_rows, n_cols, n_steps, steps_per_batch):
    n_blocks = n_steps
    while n_rows % (n_blocks * V7X_BF16_SUBLANES):
        n_blocks //= 2
    repeat = n_steps // n_blocks
    return pl.BlockSpec((n_rows // n_blocks, n_cols),
                        lambda b, i: ((b * steps_per_batch + i) // repeat, 0))


def _mod_spec(B, D, which):
    return pl.BlockSpec((B, D), lambda b, i: (0, which))


def _in_proj(x, mod, w_in, rope_q, rope_k, sgu_ln_g, sgu_ln_b, sgu_w, sgu_bt, later_weights):
    B, S, D = x.shape
    P = w_in.shape[1]
    rows = PROJ_ROWS
    steps_per_batch = S // rows
    tok = lambda width: pl.BlockSpec((1, rows, width), lambda b, i: (b, i, 0))
    rope_spec = pl.BlockSpec((3, rows, V7X_LANES), lambda b, i: (0, i, 0))
    out_sd = jax.ShapeDtypeStruct((B, S, QKV_COLS), _BF16)
    cast_specs = [_row_block_spec(w.shape[0], w.shape[1], B * steps_per_batch, steps_per_batch)
                  for w in later_weights]
    outs = pl.pallas_call(
        _in_proj_kernel,
        grid=(B, steps_per_batch),
        in_specs=[
            tok(D), _mod_spec(B, D, MOD_SCALE1), _mod_spec(B, D, MOD_SHIFT1),
            _resident((D, P)), rope_spec, rope_spec,
            _resident((1, SGU_WIDTH)), _resident((1, SGU_WIDTH)),
            _resident((SGU_GROUPS, CHUNK, CHUNK)), _resident((CHUNK, SGU_GROUPS)),
        ] + cast_specs,
        out_specs=[tok(QKV_COLS), tok(QKV_COLS), tok(QKV_COLS), tok(SGU_WIDTH)] + cast_specs,
        out_shape=[out_sd, out_sd, out_sd, jax.ShapeDtypeStruct((B, S, SGU_WIDTH), _BF16)]
        + [jax.ShapeDtypeStruct(w.shape, _BF16) for w in later_weights],
        compiler_params=pltpu.CompilerParams(
            dimension_semantics=("arbitrary", "arbitrary"),
            vmem_limit_bytes=V7X_VMEM_LIMIT_BYTES),
        name="in_proj",
    )(x, mod, mod, w_in, rope_q, rope_k, sgu_ln_g, sgu_ln_b, sgu_w, sgu_bt, *later_weights)
    return outs[:4], outs[4:]


def _attn_setup(lq1_ref, lk1_ref, lq2_ref, lk2_ref, g_ref, lam_init):
    tq = ATTN_Q_ROWS
    lam = (jnp.exp(jnp.sum(lq1_ref[...] * lk1_ref[...], axis=-1, keepdims=True))
           - jnp.exp(jnp.sum(lq2_ref[...] * lk2_ref[...], axis=-1, keepdims=True))
           + lam_init)
    lane = lax.broadcasted_iota(jnp.int32, (1, V7X_LANES), 1)
    causal = (lax.broadcasted_iota(jnp.int32, (tq, tq), 1)
              <= lax.broadcasted_iota(jnp.int32, (tq, tq), 0))
    gain = g_ref[...] * (1.0 - lam_init)
    return lam, lane, lane < DIFF_HEAD_DIM, causal, gain


def _subln_store(o_ref, i, h, o, gain):
    tq = ATTN_Q_ROWS
    y = o * lax.rsqrt(jnp.mean(o * o, axis=-1, keepdims=True) + LN_EPS) * gain
    o_ref[0, i * tq:(i + 1) * tq, h * DIFF_VDIM:(h + 1) * DIFF_VDIM] = y.astype(_BF16)


def _diff_attn_fast_kernel(lq1_ref, lk1_ref, lq2_ref, lk2_ref, g_ref, q_ref, k_ref, v_ref,
                           o_ref, lrange_ref, kt_ref, *, lam_init):
    S = q_ref.shape[1]
    n_heads = q_ref.shape[2] // DIFF_VDIM
    tq = ATTN_Q_ROWS
    lam, _, first, causal, gain = _attn_setup(lq1_ref, lk1_ref, lq2_ref, lk2_ref, g_ref, lam_init)
    neg = jnp.finfo(_F32).min

    def exp_scores(h, i):
        r0, n = i * tq, (i + 1) * tq
        cols = slice(h * DIFF_VDIM, (h + 1) * DIFF_VDIM)
        q = q_ref[0, r0:n, cols]
        zero = jnp.zeros_like(q)
        kt_ref[cols, r0:n] = k_ref[0, r0:n, cols].astype(_F32).T.astype(_BF16)
        out = []
        for qc in (jnp.where(first, q, zero), jnp.where(first, zero, q)):
            s = jnp.dot(qc, kt_ref[cols, 0:n], preferred_element_type=_F32)
            p = jnp.exp2(jnp.where(causal, s[:, r0:n], neg))
            if i > 0:
                p = jnp.concatenate([jnp.exp2(s[:, 0:r0]), p], axis=1)
            out.append((p.astype(_BF16), jnp.sum(p, axis=-1, keepdims=True)))
        return out

    stages = [(h, i) for h in range(n_heads) for i in range(S // tq)]
    lmax = jnp.zeros((tq, 1), _F32)
    lmin = jnp.full((tq, 1), jnp.inf, _F32)
    e_next = exp_scores(*stages[0])
    for idx, (h, i) in enumerate(stages):
        (p1, l1), (p2, l2) = e_next
        if idx + 1 < len(stages):
            e_next = exp_scores(*stages[idx + 1])
        n = (i + 1) * tq
        lmax = jnp.maximum(lmax, jnp.maximum(l1, l2))
        lmin = jnp.minimum(lmin, jnp.minimum(l1, l2))
        a = p1 - (lam * l1 / l2).astype(_BF16) * p2
        kh = n // 2
        vh = slice(h * DIFF_VDIM, (h + 1) * DIFF_VDIM)
        o = (jnp.dot(a[:, 0:kh], v_ref[0, 0:kh, vh], preferred_element_type=_F32)
             + jnp.dot(a[:, kh:n], v_ref[0, kh:n, vh], preferred_element_type=_F32))
        _subln_store(o_ref, i, h, o[:, 0:DIFF_VDIM] * (1.0 / l1), gain)
    top = jnp.max(lmax, axis=0, keepdims=True)
    bot = jnp.min(lmin, axis=0, keepdims=True)
    upper = lax.broadcasted_iota(jnp.int32, (V7X_SUBLANES, V7X_LANES), 0) < LRANGE_MIN_ROW
    lrange_ref[0, 0] = jnp.where(upper, top, bot)


def _diff_attn_exact_kernel(lq1_ref, lk1_ref, lq2_ref, lk2_ref, g_ref, q_ref, k_ref, v_ref, o_ref,
                            vaug_ref, *, lam_init):
    S = q_ref.shape[1]
    tq = ATTN_Q_ROWS
    lam, lane, first, causal, gain = _attn_setup(lq1_ref, lk1_ref, lq2_ref, lk2_ref, g_ref, lam_init)
    neg = jnp.finfo(_F32).min

    vaug_ref[:, 0:DIFF_VDIM] = v_ref[0]
    vaug_ref[:, DIFF_VDIM:] = jnp.broadcast_to(jnp.where(lane == 0, 1.0, 0.0), (S, V7X_LANES)).astype(_BF16)

    def scores(i):
        n = (i + 1) * tq
        q = q_ref[0, i * tq:n, :]
        zero = jnp.zeros_like(q)
        return [lax.dot_general(qc, k_ref[0, 0:n, :], _NT, preferred_element_type=_F32)
                for qc in (jnp.where(first, q, zero), jnp.where(first, zero, q))]

    def weighted_values(s, i):
        r0, n = i * tq, (i + 1) * tq
        s_d = jnp.where(causal, s[:, r0:n], neg)
        m = jnp.max(s_d, axis=-1, keepdims=True)
        if i > 0:
            m = jnp.maximum(m, jnp.max(s[:, 0:r0], axis=-1, keepdims=True))
            p = jnp.concatenate([jnp.exp2(s[:, 0:r0] - m), jnp.exp2(s_d - m)], axis=1)
        else:
            p = jnp.exp2(s_d - m)
        return jnp.dot(p.astype(_BF16), vaug_ref[0:n, :], preferred_element_type=_F32)

    n_blocks = S // tq
    s_next = scores(0)
    for i in range(n_blocks):
        s_cur = s_next
        if i + 1 < n_blocks:
            s_next = scores(i + 1)
        acc1 = weighted_values(s_cur[0], i)
        acc2 = weighted_values(s_cur[1], i)
        r1 = 1.0 / acc1[:, DIFF_VDIM:DIFF_VDIM + 1]
        r2 = lam / acc2[:, DIFF_VDIM:DIFF_VDIM + 1]
        _subln_store(o_ref, i, 0, acc1[:, 0:DIFF_VDIM] * r1 - acc2[:, 0:DIFF_VDIM] * r2, gain)


def _diff_attn(q, k, v, lq1, lk1, lq2, lk2, subln_g, lam_init):
    B, S, _ = q.shape
    head = pl.BlockSpec((1, S, DIFF_VDIM), lambda b, h: (b, 0, h))
    vec = _resident((1, DIFF_HEAD_DIM))
    in_specs = [vec, vec, vec, vec, _resident((1, DIFF_VDIM)), head, head, head]
    params = pltpu.CompilerParams(dimension_semantics=("arbitrary", "arbitrary"),
                                  vmem_limit_bytes=V7X_VMEM_LIMIT_BYTES)
    out_sd = jax.ShapeDtypeStruct((B, S, DIFF_WIDTH), _BF16)
    args = (lq1, lk1, lq2, lk2, subln_g, q, k, v)
    wide = DIFF_VDIM + V7X_LANES

    hps = ATTN_FAST_HEADS_PER_STEP
    heads = pl.BlockSpec((1, S, hps * DIFF_VDIM), lambda b, h: (b, 0, h))
    fast, lrange = pl.pallas_call(
        partial(_diff_attn_fast_kernel, lam_init=lam_init),
        grid=(B, N_DIFF_HEADS // hps),
        in_specs=in_specs[:5] + [heads] * 3,
        out_specs=[heads, pl.BlockSpec((1, 1, V7X_SUBLANES, V7X_LANES), lambda b, h: (b, h, 0, 0))],
        out_shape=[out_sd, jax.ShapeDtypeStruct((B, N_DIFF_HEADS // hps, V7X_SUBLANES, V7X_LANES), _F32)],
        scratch_shapes=[pltpu.VMEM((hps * DIFF_VDIM, S), _BF16)],
        compiler_params=params,
        name="diff_attn_fast",
    )(*args)

    def exact():
        return pl.pallas_call(
            partial(_diff_attn_exact_kernel, lam_init=lam_init),
            grid=(B, N_DIFF_HEADS),
            in_specs=in_specs,
            out_specs=head,
            out_shape=out_sd,
            scratch_shapes=[pltpu.VMEM((S, wide), _BF16)],
            compiler_params=params,
            name="diff_attn_exact",
        )(*args)

    in_range = ((jnp.max(lrange[:, :, 0, 0]) <= ATTN_ROWSUM_LIMIT)
                & (jnp.min(lrange[:, :, LRANGE_MIN_ROW, 0]) >= 1.0 / ATTN_ROWSUM_LIMIT))
    return lax.cond(in_range, lambda: fast, exact)


def _out_ffn_kernel(x_ref, attn_ref, sgu_ref, gate1_ref, shift2_ref, scale2_ref, gate2_ref,
                    wo_ref, ln1g_ref, ln1b_ref, wg_ref, wu_ref, wd_ref, ln2g_ref, ln2b_ref,
                    o_ref):
    rows = x_ref.shape[1]
    b = pl.program_id(0)
    gate1, shift2, scale2, gate2 = [r[pl.ds(b, 1), :]
                                    for r in (gate1_ref, shift2_ref, scale2_ref, gate2_ref)]
    slabs = [slice(r, r + FFN_SLAB_ROWS) for r in range(0, rows, FFN_SLAB_ROWS)]
    mix = [jnp.dot(jnp.concatenate([attn_ref[0, rs, :], sgu_ref[0, rs, :]], axis=1), wo_ref[...],
                   preferred_element_type=_F32) for rs in slabs]
    for rs, mix_s in zip(slabs, mix):
        x1 = _layernorm_rows(ALPHA * x_ref[0, rs, :] + gate1 * mix_s, ln1g_ref[...], ln1b_ref[...])
        h2 = (x1 * (1.0 + scale2) + shift2).astype(_BF16)
        g = jnp.dot(h2, wg_ref[...], preferred_element_type=_F32)
        u = jnp.dot(h2, wu_ref[...], preferred_element_type=_F32)
        a = (g * jax.nn.sigmoid(g) * u).astype(_BF16)
        ffn = jnp.dot(a, wd_ref[...], preferred_element_type=_F32)
        o_ref[0, rs, :] = _layernorm_rows(ALPHA * x1 + gate2 * ffn, ln2g_ref[...], ln2b_ref[...])


def _out_ffn(x, attn, sgu, mod, w_o, ln1_g, ln1_b, w_gate, w_up, w_down, ln2_g, ln2_b):
    B, S, D = x.shape
    F = w_gate.shape[1]
    rows = FFN_ROWS
    tok = lambda width: pl.BlockSpec((1, rows, width), lambda b, i: (b, i, 0))
    vec = _resident((1, D))
    return pl.pallas_call(
        _out_ffn_kernel,
        grid=(B, S // rows),
        in_specs=[
            tok(D), tok(DIFF_WIDTH), tok(SGU_WIDTH),
            _mod_spec(B, D, MOD_GATE1), _mod_spec(B, D, MOD_SHIFT2),
            _mod_spec(B, D, MOD_SCALE2), _mod_spec(B, D, MOD_GATE2),
            _resident((D, D)), vec, vec, _resident((D, F)), _resident((D, F)), _resident((F, D)),
            vec, vec,
        ],
        out_specs=tok(D),
        out_shape=jax.ShapeDtypeStruct((B, S, D), x.dtype),
        compiler_params=pltpu.CompilerParams(
            dimension_semantics=("arbitrary", "arbitrary"),
            vmem_limit_bytes=V7X_VMEM_LIMIT_BYTES),
        name="out_ffn",
    )(x, attn, sgu, mod, mod, mod, mod, w_o, ln1_g, ln1_b, w_gate, w_up, w_down, ln2_g, ln2_b)


def _rope_tables(seq, scale):
    half = ROT_DIM // 2
    inv_freq = np.float32(ROPE_THETA) ** (-np.arange(half, dtype=np.float32) * np.float32(2.0 / ROT_DIM))
    ang = np.arange(seq, dtype=np.float32)[:, None] * inv_freq[None, :]
    cos, sin = np.cos(ang), np.sin(ang)
    pad = np.zeros((seq, DIFF_HEAD_DIM - ROT_DIM), np.float32)
    zer = np.zeros((seq, half), np.float32)
    c_tab = np.concatenate([cos, cos, pad + 1.0], axis=-1)
    s_lo = np.concatenate([-sin, zer, pad], axis=-1)
    s_hi = np.concatenate([zer, sin, pad], axis=-1)
    tabs = np.stack([c_tab, s_lo, s_hi]).astype(np.float32) * np.float32(scale)
    return jnp.asarray(np.tile(tabs, (1, 1, V7X_LANES // DIFF_HEAD_DIM)))


def kernel(x, c, ada_w, ada_b, w_in, lambda_q1, lambda_k1, lambda_q2, lambda_k2, subln_g,
           sgu_ln_g, sgu_ln_b, sgu_w, sgu_b, w_o, ln1_g, ln1_b, w_gate, w_up, w_down,
           ln2_g, ln2_b):
    B, S, D = x.shape
    rope_q = _rope_tables(S, DIFF_HEAD_DIM ** -0.5 * math.log2(math.e))
    rope_k = _rope_tables(S, 1.0)
    for l in range(DEPTH):
        lam_init = 0.8 - 0.6 * math.exp(-0.3 * l)
        mod, w_in16 = _adaln_mod(c, ada_w[l], ada_b[l][None, :], w_in[l])
        (q, k, v, sgu), (w_o16, w_gate16, w_up16, w_down16) = _in_proj(
            x, mod, w_in16, rope_q, rope_k,
            sgu_ln_g[l][None, :], sgu_ln_b[l][None, :], sgu_w[l], sgu_b[l].T,
            (w_o[l], w_gate[l], w_up[l], w_down[l]))
        attn = _diff_attn(q, k, v, lambda_q1[l][None, :], lambda_k1[l][None, :],
                          lambda_q2[l][None, :], lambda_k2[l][None, :],
                          subln_g[l][None, :], lam_init)
        x = _out_ffn(x, attn, sgu, mod, w_o16, ln1_g[l][None, :], ln1_b[l][None, :],
                     w_gate16, w_up16, w_down16, ln2_g[l][None, :], ln2_b[l][None, :])
    return x
```

```python
import math
from functools import partial

import numpy as np
import jax
import jax.numpy as jnp
from jax import lax
from jax.experimental import pallas as pl
from jax.experimental.pallas import tpu as pltpu

D_MODEL = 1024
N_DIFF_HEADS = 4
DIFF_HEAD_DIM = 64
DIFF_VDIM = 2 * DIFF_HEAD_DIM
DIFF_WIDTH = N_DIFF_HEADS * DIFF_VDIM
SGU_WIDTH = D_MODEL - DIFF_WIDTH
SGU_GROUPS = 4
SGU_GROUP_DIM = SGU_WIDTH // SGU_GROUPS
CHUNK = 128
ROT_DIM = DIFF_HEAD_DIM // 4
ROPE_THETA = 500000.0
DEPTH = 1
ALPHA = (2 * DEPTH) ** 0.25
LN_EPS = 1e-5
QKV_COLS = DIFF_WIDTH
MOD_SHIFT1, MOD_SCALE1, MOD_GATE1, MOD_SHIFT2, MOD_SCALE2, MOD_GATE2 = range(6)

V7X_LANES = 128
V7X_SUBLANES = 8
V7X_BF16_SUBLANES = 2 * V7X_SUBLANES
V7X_VMEM_LIMIT_BYTES = 56 * 1024 * 1024

PROJ_ROWS = 1024
FFN_ROWS = 1024
FFN_SLAB_ROWS = 256
ATTN_Q_ROWS = 256
ATTN_FAST_HEADS_PER_STEP = 2
ATTN_ROWSUM_LIMIT = 2.0 ** 60
LRANGE_MIN_ROW = V7X_SUBLANES // 2
ADA_COLS = 3072

_BF16 = jnp.bfloat16
_F32 = jnp.float32
_NT = (((1,), (1,)), ((), ()))


def _resident(shape):
    zeros = (0,) * len(shape)
    return pl.BlockSpec(shape, lambda *_: zeros, pipeline_mode=pl.Buffered(1))


def _layernorm_rows(y, g, b):
    mu = jnp.mean(y, axis=-1, keepdims=True)
    d = y - mu
    var = jnp.mean(d * d, axis=-1, keepdims=True)
    return d * lax.rsqrt(var + LN_EPS) * g + b


def _adaln_kernel(c_ref, w_ref, b_ref, win_ref, o_ref, win16_ref):
    c = c_ref[...]
    c_act = (c * jax.nn.sigmoid(c)).astype(_BF16)
    acc = jnp.dot(c_act, w_ref[...].astype(_BF16), preferred_element_type=_F32)
    o_ref[...] = acc + b_ref[...]
    win16_ref[...] = win_ref[...].astype(_BF16)


def _adaln_mod(c, ada_w, ada_b, w_in):
    B, D = c.shape
    N = ada_w.shape[1]
    steps = N // ADA_COLS
    win_spec = pl.BlockSpec((w_in.shape[0] // steps, w_in.shape[1]), lambda j: (j, 0))
    return pl.pallas_call(
        _adaln_kernel,
        grid=(steps,),
        in_specs=[
            pl.BlockSpec((B, D), lambda j: (0, 0)),
            pl.BlockSpec((D, ADA_COLS), lambda j: (0, j)),
            pl.BlockSpec((1, ADA_COLS), lambda j: (0, j)),
            win_spec,
        ],
        out_specs=[pl.BlockSpec((B, ADA_COLS), lambda j: (0, j)), win_spec],
        out_shape=[jax.ShapeDtypeStruct((B, N), _F32), jax.ShapeDtypeStruct(w_in.shape, _BF16)],
        compiler_params=pltpu.CompilerParams(
            dimension_semantics=("arbitrary",), vmem_limit_bytes=V7X_VMEM_LIMIT_BYTES),
        name="adaln_mod",
    )(c, ada_w, ada_b, w_in)


def _rope(t, c_tab, s_lo, s_hi):
    up = pltpu.roll(t, V7X_LANES - ROT_DIM // 2, 1)
    dn = pltpu.roll(t, ROT_DIM // 2, 1)
    return t * c_tab + up * s_lo + dn * s_hi


def _in_proj_kernel(x_ref, scale_ref, shift_ref, w_ref, ropeq_ref, ropek_ref,
                    lng_ref, lnb_ref, sw_ref, sbt_ref, *rest):
    n_cast = (len(rest) - 4) // 2
    cast_in, (q_ref, k_ref, v_ref, sgu_ref), cast_out = (
        rest[:n_cast], rest[n_cast:n_cast + 4], rest[n_cast + 4:])
    for src, dst in zip(cast_in, cast_out):
        dst[...] = src[...].astype(_BF16)

    rows = x_ref.shape[1]
    n_chunks = rows // CHUNK
    b = pl.program_id(0)
    scale, shift = scale_ref[pl.ds(b, 1), :], shift_ref[pl.ds(b, 1), :]
    h = (x_ref[0] * (1.0 + scale) + shift).astype(_BF16)

    def proj(lo, width):
        return jnp.dot(h, w_ref[:, lo:lo + width], preferred_element_type=_F32)

    o3 = 3 * QKV_COLS
    zu = proj(o3, SGU_WIDTH)
    zv = proj(o3 + SGU_WIDTH, SGU_WIDTH)
    inv_sqrt2 = 1.0 / math.sqrt(2.0)
    u = 0.5 * zu * (1.0 + lax.erf(zu * inv_sqrt2))
    vv = 0.5 * zv * (1.0 + lax.erf(zv * inv_sqrt2))
    vv = _layernorm_rows(vv, lng_ref[...], lnb_ref[...]).astype(_BF16)

    pos = pl.ds(pl.multiple_of(pl.program_id(1) * rows, rows), rows)
    for out_ref, tabs, base in ((q_ref, ropeq_ref, 0), (k_ref, ropek_ref, QKV_COLS)):
        c_tab, s_lo, s_hi = tabs[0, pos, :], tabs[1, pos, :], tabs[2, pos, :]
        t = proj(base, QKV_COLS)
        for s in range(QKV_COLS // V7X_LANES):
            sl = slice(s * V7X_LANES, (s + 1) * V7X_LANES)
            out_ref[0, :, sl] = _rope(t[:, sl], c_tab, s_lo, s_hi).astype(_BF16)

    v_ref[0] = proj(2 * QKV_COLS, QKV_COLS).astype(_BF16)

    tri = (lax.broadcasted_iota(jnp.int32, (CHUNK, CHUNK), 0)
           >= lax.broadcasted_iota(jnp.int32, (CHUNK, CHUNK), 1))
    for g in range(SGU_GROUPS):
        cols = slice(g * SGU_GROUP_DIM, (g + 1) * SGU_GROUP_DIM)
        w_g = jnp.where(tri, sw_ref[g], 0.0).astype(_BF16)
        b_g = sbt_ref[:, g:g + 1]
        v_g = jnp.concatenate([vv[n * CHUNK:(n + 1) * CHUNK, cols] for n in range(n_chunks)], axis=1)
        sv = jnp.dot(w_g, v_g, preferred_element_type=_F32) + b_g
        for n in range(n_chunks):
            rws = slice(n * CHUNK, (n + 1) * CHUNK)
            sgu_ref[0, rws, cols] = (
                u[rws, cols] * sv[:, n * SGU_GROUP_DIM:(n + 1) * SGU_GROUP_DIM]).astype(_BF16)


def _row_block_spec(n_rows, n_cols, n_steps, steps_per_batch):
    n_blocks = n_steps
    while n_rows % (n_blocks * V7X_BF16_SUBLANES):
        n_blocks //= 2
    repeat = n_steps // n_blocks
    return pl.BlockSpec((n_rows // n_blocks, n_cols),
                        lambda b, i: ((b * steps_per_batch + i) // repeat, 0))


def _mod_spec(B, D, which):
    return pl.BlockSpec((B, D), lambda b, i: (0, which))


def _in_proj(x, mod, w_in, rope_q, rope_k, sgu_ln_g, sgu_ln_b, sgu_w, sgu_bt, later_weights):
    B, S, D = x.shape
    P = w_in.shape[1]
    rows = PROJ_ROWS
    steps_per_batch = S // rows
    tok = lambda width: pl.BlockSpec((1, rows, width), lambda b, i: (b, i, 0))
    rope_spec = _resident((3, S, V7X_LANES))
    out_sd = jax.ShapeDtypeStruct((B, S, QKV_COLS), _BF16)
    cast_specs = [_row_block_spec(w.shape[0], w.shape[1], B * steps_per_batch, steps_per_batch)
                  for w in later_weights]
    outs = pl.pallas_call(
        _in_proj_kernel,
        grid=(B, steps_per_batch),
        in_specs=[
            tok(D), _mod_spec(B, D, MOD_SCALE1), _mod_spec(B, D, MOD_SHIFT1),
            _resident((D, P)), rope_spec, rope_spec,
            _resident((1, SGU_WIDTH)), _resident((1, SGU_WIDTH)),
            _resident((SGU_GROUPS, CHUNK, CHUNK)), _resident((CHUNK, SGU_GROUPS)),
        ] + cast_specs,
        out_specs=[tok(QKV_COLS), tok(QKV_COLS), tok(QKV_COLS), tok(SGU_WIDTH)] + cast_specs,
        out_shape=[out_sd, out_sd, out_sd, jax.ShapeDtypeStruct((B, S, SGU_WIDTH), _BF16)]
        + [jax.ShapeDtypeStruct(w.shape, _BF16) for w in later_weights],
        compiler_params=pltpu.CompilerParams(
            dimension_semantics=("arbitrary", "arbitrary"),
            vmem_limit_bytes=V7X_VMEM_LIMIT_BYTES),
        name="in_proj",
    )(x, mod, mod, w_in, rope_q, rope_k, sgu_ln_g, sgu_ln_b, sgu_w, sgu_bt, *later_weights)
    return outs[:4], outs[4:]


def _attn_setup(lq1_ref, lk1_ref, lq2_ref, lk2_ref, g_ref, lam_init):
    tq = ATTN_Q_ROWS
    lam = (jnp.exp(jnp.sum(lq1_ref[...] * lk1_ref[...], axis=-1, keepdims=True))
           - jnp.exp(jnp.sum(lq2_ref[...] * lk2_ref[...], axis=-1, keepdims=True))
           + lam_init)
    lane = lax.broadcasted_iota(jnp.int32, (1, V7X_LANES), 1)
    causal = (lax.broadcasted_iota(jnp.int32, (tq, tq), 1)
              <= lax.broadcasted_iota(jnp.int32, (tq, tq), 0))
    gain = g_ref[...] * (1.0 - lam_init)
    return lam, lane, lane < DIFF_HEAD_DIM, causal, gain


def _subln_store(o_ref, i, h, o, gain):
    tq = ATTN_Q_ROWS
    y = o * lax.rsqrt(jnp.mean(o * o, axis=-1, keepdims=True) + LN_EPS) * gain
    o_ref[0, i * tq:(i + 1) * tq, h * DIFF_VDIM:(h + 1) * DIFF_VDIM] = y.astype(_BF16)


def _diff_attn_fast_kernel(lq1_ref, lk1_ref, lq2_ref, lk2_ref, g_ref, q_ref, k_ref, v_ref,
                           o_ref, lrange_ref, kt_ref, *, lam_init):
    S = q_ref.shape[1]
    n_heads = q_ref.shape[2] // DIFF_VDIM
    tq = ATTN_Q_ROWS
    lam, _, first, causal, gain = _attn_setup(lq1_ref, lk1_ref, lq2_ref, lk2_ref, g_ref, lam_init)
    neg = jnp.finfo(_F32).min

    def exp_scores(h, i):
        r0, n = i * tq, (i + 1) * tq
        cols = slice(h * DIFF_VDIM, (h + 1) * DIFF_VDIM)
        q = q_ref[0, r0:n, cols]
        zero = jnp.zeros_like(q)
        kt_ref[cols, r0:n] = k_ref[0, r0:n, cols].astype(_F32).T.astype(_BF16)
        out = []
        for qc in (jnp.where(first, q, zero), jnp.where(first, zero, q)):
            s = jnp.dot(qc, kt_ref[cols, 0:n], preferred_element_type=_F32)
            p = jnp.exp2(jnp.where(causal, s[:, r0:n], neg))
            if i > 0:
                p = jnp.concatenate([jnp.exp2(s[:, 0:r0]), p], axis=1)
            out.append((p.astype(_BF16), jnp.sum(p, axis=-1, keepdims=True)))
        return out

    stages = [(h, i) for h in range(n_heads) for i in range(S // tq)]
    lmax = jnp.zeros((tq, 1), _F32)
    lmin = jnp.full((tq, 1), jnp.inf, _F32)
    e_next = exp_scores(*stages[0])
    for idx, (h, i) in enumerate(stages):
        (p1, l1), (p2, l2) = e_next
        if idx + 1 < len(stages):
            e_next = exp_scores(*stages[idx + 1])
        n = (i + 1) * tq
        lmax = jnp.maximum(lmax, jnp.maximum(l1, l2))
        lmin = jnp.minimum(lmin, jnp.minimum(l1, l2))
        a = p1 - (lam * l1 / l2).astype(_BF16) * p2
        kh = n // 2
        vh = slice(h * DIFF_VDIM, (h + 1) * DIFF_VDIM)
        o = (jnp.dot(a[:, 0:kh], v_ref[0, 0:kh, vh], preferred_element_type=_F32)
             + jnp.dot(a[:, kh:n], v_ref[0, kh:n, vh], preferred_element_type=_F32))
        _subln_store(o_ref, i, h, o[:, 0:DIFF_VDIM] * (1.0 / l1), gain)
    top = jnp.max(lmax, axis=0, keepdims=True)
    bot = jnp.min(lmin, axis=0, keepdims=True)
    upper = lax.broadcasted_iota(jnp.int32, (V7X_SUBLANES, V7X_LANES), 0) < LRANGE_MIN_ROW
    lrange_ref[0, 0] = jnp.where(upper, top, bot)


def _diff_attn_exact_kernel(lq1_ref, lk1_ref, lq2_ref, lk2_ref, g_ref, q_ref, k_ref, v_ref, o_ref,
                            vaug_ref, *, lam_init):
    S = q_ref.shape[1]
    tq = ATTN_Q_ROWS
    lam, lane, first, causal, gain = _attn_setup(lq1_ref, lk1_ref, lq2_ref, lk2_ref, g_ref, lam_init)
    neg = jnp.finfo(_F32).min

    vaug_ref[:, 0:DIFF_VDIM] = v_ref[0]
    vaug_ref[:, DIFF_VDIM:] = jnp.broadcast_to(jnp.where(lane == 0, 1.0, 0.0), (S, V7X_LANES)).astype(_BF16)

    def scores(i):
        n = (i + 1) * tq
        q = q_ref[0, i * tq:n, :]
        zero = jnp.zeros_like(q)
        return [lax.dot_general(qc, k_ref[0, 0:n, :], _NT, preferred_element_type=_F32)
                for qc in (jnp.where(first, q, zero), jnp.where(first, zero, q))]

    def weighted_values(s, i):
        r0, n = i * tq, (i + 1) * tq
        s_d = jnp.where(causal, s[:, r0:n], neg)
        m = jnp.max(s_d, axis=-1, keepdims=True)
        if i > 0:
            m = jnp.maximum(m, jnp.max(s[:, 0:r0], axis=-1, keepdims=True))
            p = jnp.concatenate([jnp.exp2(s[:, 0:r0] - m), jnp.exp2(s_d - m)], axis=1)
        else:
            p = jnp.exp2(s_d - m)
        return jnp.dot(p.astype(_BF16), vaug_ref[0:n, :], preferred_element_type=_F32)

    n_blocks = S // tq
    s_next = scores(0)
    for i in range(n_blocks):
        s_cur = s_next
        if i + 1 < n_blocks:
            s_next = scores(i + 1)
        acc1 = weighted_values(s_cur[0], i)
        acc2 = weighted_values(s_cur[1], i)
        r1 = 1.0 / acc1[:, DIFF_VDIM:DIFF_VDIM + 1]
        r2 = lam / acc2[:, DIFF_VDIM:DIFF_VDIM + 1]
        _subln_store(o_ref, i, 0, acc1[:, 0:DIFF_VDIM] * r1 - acc2[:, 0:DIFF_VDIM] * r2, gain)


def _diff_attn(q, k, v, lq1, lk1, lq2, lk2, subln_g, lam_init):
    B, S, _ = q.shape
    head = pl.BlockSpec((1, S, DIFF_VDIM), lambda b, h: (b, 0, h))
    vec = _resident((1, DIFF_HEAD_DIM))
    in_specs = [vec, vec, vec, vec, _resident((1, DIFF_VDIM)), head, head, head]
    params = pltpu.CompilerParams(dimension_semantics=("arbitrary", "arbitrary"),
                                  vmem_limit_bytes=V7X_VMEM_LIMIT_BYTES)
    out_sd = jax.ShapeDtypeStruct((B, S, DIFF_WIDTH), _BF16)
    args = (lq1, lk1, lq2, lk2, subln_g, q, k, v)
    wide = DIFF_VDIM + V7X_LANES

    hps = ATTN_FAST_HEADS_PER_STEP
    heads = pl.BlockSpec((1, S, hps * DIFF_VDIM), lambda b, h: (b, 0, h))
    fast, lrange = pl.pallas_call(
        partial(_diff_attn_fast_kernel, lam_init=lam_init),
        grid=(B, N_DIFF_HEADS // hps),
        in_specs=in_specs[:5] + [heads] * 3,
        out_specs=[heads, pl.BlockSpec((1, 1, V7X_SUBLANES, V7X_LANES), lambda b, h: (b, h, 0, 0))],
        out_shape=[out_sd, jax.ShapeDtypeStruct((B, N_DIFF_HEADS // hps, V7X_SUBLANES, V7X_LANES), _F32)],
        scratch_shapes=[pltpu.VMEM((hps * DIFF_VDIM, S), _BF16)],
        compiler_params=params,
        name="diff_attn_fast",
    )(*args)

    def exact():
        return pl.pallas_call(
            partial(_diff_attn_exact_kernel, lam_init=lam_init),
            grid=(B, N_DIFF_HEADS),
            in_specs=in_specs,
            out_specs=head,
            out_shape=out_sd,
            scratch_shapes=[pltpu.VMEM((S, wide), _BF16)],
            compiler_params=params,
            name="diff_attn_exact",
        )(*args)

    in_range = ((jnp.max(lrange[:, :, 0, 0]) <= ATTN_ROWSUM_LIMIT)
                & (jnp.min(lrange[:, :, LRANGE_MIN_ROW, 0]) >= 1.0 / ATTN_ROWSUM_LIMIT))
    return lax.cond(in_range, lambda: fast, exact)


def _out_ffn_kernel(x_ref, attn_ref, sgu_ref, gate1_ref, shift2_ref, scale2_ref, gate2_ref,
                    wo_ref, ln1g_ref, ln1b_ref, wg_ref, wu_ref, wd_ref, ln2g_ref, ln2b_ref,
                    o_ref):
    rows = x_ref.shape[1]
    b = pl.program_id(0)
    gate1, shift2, scale2, gate2 = [r[pl.ds(b, 1), :]
                                    for r in (gate1_ref, shift2_ref, scale2_ref, gate2_ref)]
    slabs = [slice(r, r + FFN_SLAB_ROWS) for r in range(0, rows, FFN_SLAB_ROWS)]
    mix = [jnp.dot(jnp.concatenate([attn_ref[0, rs, :], sgu_ref[0, rs, :]], axis=1), wo_ref[...],
                   preferred_element_type=_F32) for rs in slabs]
    for rs, mix_s in zip(slabs, mix):
        x1 = _layernorm_rows(ALPHA * x_ref[0, rs, :] + gate1 * mix_s, ln1g_ref[...], ln1b_ref[...])
        h2 = (x1 * (1.0 + scale2) + shift2).astype(_BF16)
        g = jnp.dot(h2, wg_ref[...], preferred_element_type=_F32)
        u = jnp.dot(h2, wu_ref[...], preferred_element_type=_F32)
        a = (g * jax.nn.sigmoid(g) * u).astype(_BF16)
        ffn = jnp.dot(a, wd_ref[...], preferred_element_type=_F32)
        o_ref[0, rs, :] = _layernorm_rows(ALPHA * x1 + gate2 * ffn, ln2g_ref[...], ln2b_ref[...])


def _out_ffn(x, attn, sgu, mod, w_o, ln1_g, ln1_b, w_gate, w_up, w_down, ln2_g, ln2_b):
    B, S, D = x.shape
    F = w_gate.shape[1]
    rows = FFN_ROWS
    tok = lambda width: pl.BlockSpec((1, rows, width), lambda b, i: (b, i, 0))
    vec = _resident((1, D))
    return pl.pallas_call(
        _out_ffn_kernel,
        grid=(B, S // rows),
        in_specs=[
            tok(D), tok(DIFF_WIDTH), tok(SGU_WIDTH),
            _mod_spec(B, D, MOD_GATE1), _mod_spec(B, D, MOD_SHIFT2),
            _mod_spec(B, D, MOD_SCALE2), _mod_spec(B, D, MOD_GATE2),
            _resident((D, D)), vec, vec, _resident((D, F)), _resident((D, F)), _resident((F, D)),
            vec, vec,
        ],
        out_specs=tok(D),
        out_shape=jax.ShapeDtypeStruct((B, S, D), x.dtype),
        compiler_params=pltpu.CompilerParams(
            dimension_semantics=("arbitrary", "arbitrary"),
            vmem_limit_bytes=V7X_VMEM_LIMIT_BYTES),
        name="out_ffn",
    )(x, attn, sgu, mod, mod, mod, mod, w_o, ln1_g, ln1_b, w_gate, w_up, w_down, ln2_g, ln2_b)


def _rope_tables(seq, scale):
    half = ROT_DIM // 2
    inv_freq = np.float32(ROPE_THETA) ** (-np.arange(half, dtype=np.float32) * np.float32(2.0 / ROT_DIM))
    ang = np.arange(seq, dtype=np.float32)[:, None] * inv_freq[None, :]
    cos, sin = np.cos(ang), np.sin(ang)
    pad = np.zeros((seq, DIFF_HEAD_DIM - ROT_DIM), np.float32)
    zer = np.zeros((seq, half), np.float32)
    c_tab = np.concatenate([cos, cos, pad + 1.0], axis=-1)
    s_lo = np.concatenate([-sin, zer, pad], axis=-1)
    s_hi = np.concatenate([zer, sin, pad], axis=-1)
    tabs = np.stack([c_tab, s_lo, s_hi]).astype(np.float32) * np.float32(scale)
    return jnp.asarray(np.tile(tabs, (1, 1, V7X_LANES // DIFF_HEAD_DIM)))


def kernel(x, c, ada_w, ada_b, w_in, lambda_q1, lambda_k1, lambda_q2, lambda_k2, subln_g,
           sgu_ln_g, sgu_ln_b, sgu_w, sgu_b, w_o, ln1_g, ln1_b, w_gate, w_up, w_down,
           ln2_g, ln2_b):
    B, S, D = x.shape
    rope_q = _rope_tables(S, DIFF_HEAD_DIM ** -0.5 * math.log2(math.e))
    rope_k = _rope_tables(S, 1.0)
    for l in range(DEPTH):
        lam_init = 0.8 - 0.6 * math.exp(-0.3 * l)
        mod, w_in16 = _adaln_mod(c, ada_w[l], ada_b[l][None, :], w_in[l])
        (q, k, v, sgu), (w_o16, w_gate16, w_up16, w_down16) = _in_proj(
            x, mod, w_in16, rope_q, rope_k,
            sgu_ln_g[l][None, :], sgu_ln_b[l][None, :], sgu_w[l], sgu_b[l].T,
            (w_o[l], w_gate[l], w_up[l], w_down[l]))
        attn = _diff_attn(q, k, v, lambda_q1[l][None, :], lambda_k1[l][None, :],
                          lambda_q2[l][None, :], lambda_k2[l][None, :],
                          subln_g[l][None, :], lam_init)
        x = _out_ffn(x, attn, sgu, mod, w_o16, ln1_g[l][None, :], ln1_b[l][None, :],
                     w_gate16, w_up16, w_down16, ln2_g[l][None, :], ln2_b[l][None, :])
    return x
```

```python
import math
from functools import partial

import numpy as np
import jax
import jax.numpy as jnp
from jax import lax
from jax.experimental import pallas as pl
from jax.experimental.pallas import tpu as pltpu

D_MODEL = 1024
N_DIFF_HEADS = 4
DIFF_HEAD_DIM = 64
DIFF_VDIM = 2 * DIFF_HEAD_DIM
DIFF_WIDTH = N_DIFF_HEADS * DIFF_VDIM
SGU_WIDTH = D_MODEL - DIFF_WIDTH
SGU_GROUPS = 4
SGU_GROUP_DIM = SGU_WIDTH // SGU_GROUPS
CHUNK = 128
ROT_DIM = DIFF_HEAD_DIM // 4
ROPE_THETA = 500000.0
DEPTH = 1
ALPHA = (2 * DEPTH) ** 0.25
LN_EPS = 1e-5
QKV_COLS = DIFF_WIDTH
MOD_SHIFT1, MOD_SCALE1, MOD_GATE1, MOD_SHIFT2, MOD_SCALE2, MOD_GATE2 = range(6)

V7X_LANES = 128
V7X_SUBLANES = 8
V7X_BF16_SUBLANES = 2 * V7X_SUBLANES
V7X_VMEM_LIMIT_BYTES = 56 * 1024 * 1024

PROJ_ROWS = 1024
FFN_ROWS = 1024
FFN_SLAB_ROWS = 256
ATTN_Q_ROWS = 256
ATTN_FAST_HEADS_PER_STEP = 2
ATTN_ROWSUM_LIMIT = 2.0 ** 60
LRANGE_MIN_ROW = V7X_SUBLANES // 2
ADA_COLS = 3072

_BF16 = jnp.bfloat16
_F32 = jnp.float32
_NT = (((1,), (1,)), ((), ()))


def _resident(shape):
    zeros = (0,) * len(shape)
    return pl.BlockSpec(shape, lambda *_: zeros, pipeline_mode=pl.Buffered(1))


def _layernorm_rows(y, g, b):
    mu = jnp.mean(y, axis=-1, keepdims=True)
    d = y - mu
    var = jnp.mean(d * d, axis=-1, keepdims=True)
    return d * lax.rsqrt(var + LN_EPS) * g + b


def _adaln_kernel(c_ref, w_ref, b_ref, o_ref):
    c = c_ref[...]
    c_act = (c * jax.nn.sigmoid(c)).astype(_BF16)
    acc = jnp.dot(c_act, w_ref[...].astype(_BF16), preferred_element_type=_F32)
    o_ref[...] = acc + b_ref[...]


def _adaln_mod(c, ada_w, ada_b):
    B, D = c.shape
    N = ada_w.shape[1]
    return pl.pallas_call(
        _adaln_kernel,
        grid=(N // ADA_COLS,),
        in_specs=[
            pl.BlockSpec((B, D), lambda j: (0, 0)),
            pl.BlockSpec((D, ADA_COLS), lambda j: (0, j)),
            pl.BlockSpec((1, ADA_COLS), lambda j: (0, j)),
        ],
        out_specs=pl.BlockSpec((B, ADA_COLS), lambda j: (0, j)),
        out_shape=jax.ShapeDtypeStruct((B, N), _F32),
        compiler_params=pltpu.CompilerParams(
            dimension_semantics=("arbitrary",), vmem_limit_bytes=V7X_VMEM_LIMIT_BYTES),
        name="adaln_mod",
    )(c, ada_w, ada_b)


def _rope(t, c_tab, s_lo, s_hi):
    up = pltpu.roll(t, V7X_LANES - ROT_DIM // 2, 1)
    dn = pltpu.roll(t, ROT_DIM // 2, 1)
    return t * c_tab + up * s_lo + dn * s_hi


def _in_proj_kernel(x_ref, scale_ref, shift_ref, w_ref, ropeq_ref, ropek_ref,
                    lng_ref, lnb_ref, sw_ref, sbt_ref, *rest):
    n_cast = (len(rest) - 5) // 2
    cast_in, (q_ref, k_ref, v_ref, sgu_ref), cast_out, w16_ref = (
        rest[:n_cast], rest[n_cast:n_cast + 4], rest[n_cast + 4:-1], rest[-1])
    for src, dst in zip(cast_in, cast_out):
        dst[...] = src[...].astype(_BF16)

    @pl.when((pl.program_id(0) == 0) & (pl.program_id(1) == 0))
    def _():
        w16_ref[...] = w_ref[...].astype(_BF16)

    rows = x_ref.shape[1]
    n_chunks = rows // CHUNK
    b = pl.program_id(0)
    scale, shift = scale_ref[pl.ds(b, 1), :], shift_ref[pl.ds(b, 1), :]
    h = (x_ref[0] * (1.0 + scale) + shift).astype(_BF16)

    def proj(lo, width):
        return jnp.dot(h, w16_ref[:, lo:lo + width], preferred_element_type=_F32)

    o3 = 3 * QKV_COLS
    zu = proj(o3, SGU_WIDTH)
    zv = proj(o3 + SGU_WIDTH, SGU_WIDTH)
    inv_sqrt2 = 1.0 / math.sqrt(2.0)
    u = 0.5 * zu * (1.0 + lax.erf(zu * inv_sqrt2))
    vv = 0.5 * zv * (1.0 + lax.erf(zv * inv_sqrt2))
    vv = _layernorm_rows(vv, lng_ref[...], lnb_ref[...]).astype(_BF16)

    for out_ref, tabs, base in ((q_ref, ropeq_ref, 0), (k_ref, ropek_ref, QKV_COLS)):
        c_tab, s_lo, s_hi = tabs[0], tabs[1], tabs[2]
        t = proj(base, QKV_COLS)
        for s in range(QKV_COLS // V7X_LANES):
            sl = slice(s * V7X_LANES, (s + 1) * V7X_LANES)
            out_ref[0, :, sl] = _rope(t[:, sl], c_tab, s_lo, s_hi).astype(_BF16)

    v_ref[0] = proj(2 * QKV_COLS, QKV_COLS).astype(_BF16)

    tri = (lax.broadcasted_iota(jnp.int32, (CHUNK, CHUNK), 0)
           >= lax.broadcasted_iota(jnp.int32, (CHUNK, CHUNK), 1))
    for g in range(SGU_GROUPS):
        cols = slice(g * SGU_GROUP_DIM, (g + 1) * SGU_GROUP_DIM)
        w_g = jnp.where(tri, sw_ref[g], 0.0).astype(_BF16)
        b_g = sbt_ref[:, g:g + 1]
        v_g = jnp.concatenate([vv[n * CHUNK:(n + 1) * CHUNK, cols] for n in range(n_chunks)], axis=1)
        sv = jnp.dot(w_g, v_g, preferred_element_type=_F32) + b_g
        for n in range(n_chunks):
            rws = slice(n * CHUNK, (n + 1) * CHUNK)
            sgu_ref[0, rws, cols] = (
                u[rws, cols] * sv[:, n * SGU_GROUP_DIM:(n + 1) * SGU_GROUP_DIM]).astype(_BF16)


def _row_block_spec(n_rows, n_cols, n_steps, steps_per_batch):
    n_blocks = n_steps
    while n_rows % (n_blocks * V7X_BF16_SUBLANES):
        n_blocks //= 2
    repeat = n_steps // n_blocks
    return pl.BlockSpec((n_rows // n_blocks, n_cols),
                        lambda b, i: ((b * steps_per_batch + i) // repeat, 0))


def _mod_spec(B, D, which):
    return pl.BlockSpec((B, D), lambda b, i: (0, which))


def _in_proj(x, mod, w_in, rope_q, rope_k, sgu_ln_g, sgu_ln_b, sgu_w, sgu_bt, later_weights):
    B, S, D = x.shape
    P = w_in.shape[1]
    rows = PROJ_ROWS
    steps_per_batch = S // rows
    tok = lambda width: pl.BlockSpec((1, rows, width), lambda b, i: (b, i, 0))
    rope_spec = pl.BlockSpec((3, rows, V7X_LANES), lambda b, i: (0, i, 0))
    out_sd = jax.ShapeDtypeStruct((B, S, QKV_COLS), _BF16)
    cast_specs = [_row_block_spec(w.shape[0], w.shape[1], B * steps_per_batch, steps_per_batch)
                  for w in later_weights]
    outs = pl.pallas_call(
        _in_proj_kernel,
        grid=(B, steps_per_batch),
        in_specs=[
            tok(D), _mod_spec(B, D, MOD_SCALE1), _mod_spec(B, D, MOD_SHIFT1),
            _resident((D, P)), rope_spec, rope_spec,
            _resident((1, SGU_WIDTH)), _resident((1, SGU_WIDTH)),
            _resident((SGU_GROUPS, CHUNK, CHUNK)), _resident((CHUNK, SGU_GROUPS)),
        ] + cast_specs,
        out_specs=[tok(QKV_COLS), tok(QKV_COLS), tok(QKV_COLS), tok(SGU_WIDTH)] + cast_specs,
        out_shape=[out_sd, out_sd, out_sd, jax.ShapeDtypeStruct((B, S, SGU_WIDTH), _BF16)]
        + [jax.ShapeDtypeStruct(w.shape, _BF16) for w in later_weights],
        scratch_shapes=[pltpu.VMEM((D, P), _BF16)],
        compiler_params=pltpu.CompilerParams(
            dimension_semantics=("arbitrary", "arbitrary"),
            vmem_limit_bytes=V7X_VMEM_LIMIT_BYTES),
        name="in_proj",
    )(x, mod, mod, w_in, rope_q, rope_k, sgu_ln_g, sgu_ln_b, sgu_w, sgu_bt, *later_weights)
    return outs[:4], outs[4:]


def _attn_setup(lq1_ref, lk1_ref, lq2_ref, lk2_ref, g_ref, lam_init):
    tq = ATTN_Q_ROWS
    lam = (jnp.exp(jnp.sum(lq1_ref[...] * lk1_ref[...], axis=-1, keepdims=True))
           - jnp.exp(jnp.sum(lq2_ref[...] * lk2_ref[...], axis=-1, keepdims=True))
           + lam_init)
    lane = lax.broadcasted_iota(jnp.int32, (1, V7X_LANES), 1)
    causal = (lax.broadcasted_iota(jnp.int32, (tq, tq), 1)
              <= lax.broadcasted_iota(jnp.int32, (tq, tq), 0))
    gain = g_ref[...] * (1.0 - lam_init)
    return lam, lane, lane < DIFF_HEAD_DIM, causal, gain


def _subln_store(o_ref, i, h, o, gain):
    tq = ATTN_Q_ROWS
    y = o * lax.rsqrt(jnp.mean(o * o, axis=-1, keepdims=True) + LN_EPS) * gain
    o_ref[0, i * tq:(i + 1) * tq, h * DIFF_VDIM:(h + 1) * DIFF_VDIM] = y.astype(_BF16)


def _diff_attn_fast_kernel(lq1_ref, lk1_ref, lq2_ref, lk2_ref, g_ref, q_ref, k_ref, v_ref,
                           o_ref, lrange_ref, kt_ref, *, lam_init):
    S = q_ref.shape[1]
    n_heads = q_ref.shape[2] // DIFF_VDIM
    tq = ATTN_Q_ROWS
    lam, _, first, causal, gain = _attn_setup(lq1_ref, lk1_ref, lq2_ref, lk2_ref, g_ref, lam_init)
    neg = jnp.finfo(_F32).min

    def exp_scores(h, i):
        r0, n = i * tq, (i + 1) * tq
        cols = slice(h * DIFF_VDIM, (h + 1) * DIFF_VDIM)
        q = q_ref[0, r0:n, cols]
        zero = jnp.zeros_like(q)
        kt_ref[cols, r0:n] = k_ref[0, r0:n, cols].astype(_F32).T.astype(_BF16)
        out = []
        for qc in (jnp.where(first, q, zero), jnp.where(first, zero, q)):
            s = jnp.dot(qc, kt_ref[cols, 0:n], preferred_element_type=_F32)
            p = jnp.exp2(jnp.where(causal, s[:, r0:n], neg))
            if i > 0:
                p = jnp.concatenate([jnp.exp2(s[:, 0:r0]), p], axis=1)
            out.append((p.astype(_BF16), jnp.sum(p, axis=-1, keepdims=True)))
        return out

    stages = [(h, i) for h in range(n_heads) for i in range(S // tq)]
    lmax = jnp.zeros((tq, 1), _F32)
    lmin = jnp.full((tq, 1), jnp.inf, _F32)
    e_next = exp_scores(*stages[0])
    for idx, (h, i) in enumerate(stages):
        (p1, l1), (p2, l2) = e_next
        if idx + 1 < len(stages):
            e_next = exp_scores(*stages[idx + 1])
        n = (i + 1) * tq
        lmax = jnp.maximum(lmax, jnp.maximum(l1, l2))
        lmin = jnp.minimum(lmin, jnp.minimum(l1, l2))
        a = p1 - (lam * l1 / l2).astype(_BF16) * p2
        kh = n // 2
        vh = slice(h * DIFF_VDIM, (h + 1) * DIFF_VDIM)
        o = (jnp.dot(a[:, 0:kh], v_ref[0, 0:kh, vh], preferred_element_type=_F32)
             + jnp.dot(a[:, kh:n], v_ref[0, kh:n, vh], preferred_element_type=_F32))
        _subln_store(o_ref, i, h, o[:, 0:DIFF_VDIM] * (1.0 / l1), gain)
    top = jnp.max(lmax, axis=0, keepdims=True)
    bot = jnp.min(lmin, axis=0, keepdims=True)
    upper = lax.broadcasted_iota(jnp.int32, (V7X_SUBLANES, V7X_LANES), 0) < LRANGE_MIN_ROW
    lrange_ref[0, 0] = jnp.where(upper, top, bot)


def _diff_attn_exact_kernel(lq1_ref, lk1_ref, lq2_ref, lk2_ref, g_ref, q_ref, k_ref, v_ref, o_ref,
                            vaug_ref, *, lam_init):
    S = q_ref.shape[1]
    tq = ATTN_Q_ROWS
    lam, lane, first, causal, gain = _attn_setup(lq1_ref, lk1_ref, lq2_ref, lk2_ref, g_ref, lam_init)
    neg = jnp.finfo(_F32).min

    vaug_ref[:, 0:DIFF_VDIM] = v_ref[0]
    vaug_ref[:, DIFF_VDIM:] = jnp.broadcast_to(jnp.where(lane == 0, 1.0, 0.0), (S, V7X_LANES)).astype(_BF16)

    def scores(i):
        n = (i + 1) * tq
        q = q_ref[0, i * tq:n, :]
        zero = jnp.zeros_like(q)
        return [lax.dot_general(qc, k_ref[0, 0:n, :], _NT, preferred_element_type=_F32)
                for qc in (jnp.where(first, q, zero), jnp.where(first, zero, q))]

    def weighted_values(s, i):
        r0, n = i * tq, (i + 1) * tq
        s_d = jnp.where(causal, s[:, r0:n], neg)
        m = jnp.max(s_d, axis=-1, keepdims=True)
        if i > 0:
            m = jnp.maximum(m, jnp.max(s[:, 0:r0], axis=-1, keepdims=True))
            p = jnp.concatenate([jnp.exp2(s[:, 0:r0] - m), jnp.exp2(s_d - m)], axis=1)
        else:
            p = jnp.exp2(s_d - m)
        return jnp.dot(p.astype(_BF16), vaug_ref[0:n, :], preferred_element_type=_F32)

    n_blocks = S // tq
    s_next = scores(0)
    for i in range(n_blocks):
        s_cur = s_next
        if i + 1 < n_blocks:
            s_next = scores(i + 1)
        acc1 = weighted_values(s_cur[0], i)
        acc2 = weighted_values(s_cur[1], i)
        r1 = 1.0 / acc1[:, DIFF_VDIM:DIFF_VDIM + 1]
        r2 = lam / acc2[:, DIFF_VDIM:DIFF_VDIM + 1]
        _subln_store(o_ref, i, 0, acc1[:, 0:DIFF_VDIM] * r1 - acc2[:, 0:DIFF_VDIM] * r2, gain)


def _diff_attn(q, k, v, lq1, lk1, lq2, lk2, subln_g, lam_init):
    B, S, _ = q.shape
    head = pl.BlockSpec((1, S, DIFF_VDIM), lambda b, h: (b, 0, h))
    vec = _resident((1, DIFF_HEAD_DIM))
    in_specs = [vec, vec, vec, vec, _resident((1, DIFF_VDIM)), head, head, head]
    params = pltpu.CompilerParams(dimension_semantics=("arbitrary", "arbitrary"),
                                  vmem_limit_bytes=V7X_VMEM_LIMIT_BYTES)
    out_sd = jax.ShapeDtypeStruct((B, S, DIFF_WIDTH), _BF16)
    args = (lq1, lk1, lq2, lk2, subln_g, q, k, v)
    wide = DIFF_VDIM + V7X_LANES

    hps = ATTN_FAST_HEADS_PER_STEP
    heads = pl.BlockSpec((1, S, hps * DIFF_VDIM), lambda b, h: (b, 0, h))
    fast, lrange = pl.pallas_call(
        partial(_diff_attn_fast_kernel, lam_init=lam_init),
        grid=(B, N_DIFF_HEADS // hps),
        in_specs=in_specs[:5] + [heads] * 3,
        out_specs=[heads, pl.BlockSpec((1, 1, V7X_SUBLANES, V7X_LANES), lambda b, h: (b, h, 0, 0))],
        out_shape=[out_sd, jax.ShapeDtypeStruct((B, N_DIFF_HEADS // hps, V7X_SUBLANES, V7X_LANES), _F32)],
        scratch_shapes=[pltpu.VMEM((hps * DIFF_VDIM, S), _BF16)],
        compiler_params=params,
        name="diff_attn_fast",
    )(*args)

    def exact():
        return pl.pallas_call(
            partial(_diff_attn_exact_kernel, lam_init=lam_init),
            grid=(B, N_DIFF_HEADS),
            in_specs=in_specs,
            out_specs=head,
            out_shape=out_sd,
            scratch_shapes=[pltpu.VMEM((S, wide), _BF16)],
            compiler_params=params,
            name="diff_attn_exact",
        )(*args)

    in_range = ((jnp.max(lrange[:, :, 0, 0]) <= ATTN_ROWSUM_LIMIT)
                & (jnp.min(lrange[:, :, LRANGE_MIN_ROW, 0]) >= 1.0 / ATTN_ROWSUM_LIMIT))
    return lax.cond(in_range, lambda: fast, exact)


def _out_ffn_kernel(x_ref, attn_ref, sgu_ref, gate1_ref, shift2_ref, scale2_ref, gate2_ref,
                    wo_ref, ln1g_ref, ln1b_ref, wg_ref, wu_ref, wd_ref, ln2g_ref, ln2b_ref,
                    o_ref):
    rows = x_ref.shape[1]
    b = pl.program_id(0)
    gate1, shift2, scale2, gate2 = [r[pl.ds(b, 1), :]
                                    for r in (gate1_ref, shift2_ref, scale2_ref, gate2_ref)]
    slabs = [slice(r, r + FFN_SLAB_ROWS) for r in range(0, rows, FFN_SLAB_ROWS)]
    mix = [jnp.dot(jnp.concatenate([attn_ref[0, rs, :], sgu_ref[0, rs, :]], axis=1), wo_ref[...],
                   preferred_element_type=_F32) for rs in slabs]
    for rs, mix_s in zip(slabs, mix):
        x1 = _layernorm_rows(ALPHA * x_ref[0, rs, :] + gate1 * mix_s, ln1g_ref[...], ln1b_ref[...])
        h2 = (x1 * (1.0 + scale2) + shift2).astype(_BF16)
        g = jnp.dot(h2, wg_ref[...], preferred_element_type=_F32)
        u = jnp.dot(h2, wu_ref[...], preferred_element_type=_F32)
        a = (g * jax.nn.sigmoid(g) * u).astype(_BF16)
        ffn = jnp.dot(a, wd_ref[...], preferred_element_type=_F32)
        o_ref[0, rs, :] = _layernorm_rows(ALPHA * x1 + gate2 * ffn, ln2g_ref[...], ln2b_ref[...])


def _out_ffn(x, attn, sgu, mod, w_o, ln1_g, ln1_b, w_gate, w_up, w_down, ln2_g, ln2_b):
    B, S, D = x.shape
    F = w_gate.shape[1]
    rows = FFN_ROWS
    tok = lambda width: pl.BlockSpec((1, rows, width), lambda b, i: (b, i, 0))
    vec = _resident((1, D))
    return pl.pallas_call(
        _out_ffn_kernel,
        grid=(B, S // rows),
        in_specs=[
            tok(D), tok(DIFF_WIDTH), tok(SGU_WIDTH),
            _mod_spec(B, D, MOD_GATE1), _mod_spec(B, D, MOD_SHIFT2),
            _mod_spec(B, D, MOD_SCALE2), _mod_spec(B, D, MOD_GATE2),
            _resident((D, D)), vec, vec, _resident((D, F)), _resident((D, F)), _resident((F, D)),
            vec, vec,
        ],
        out_specs=tok(D),
        out_shape=jax.ShapeDtypeStruct((B, S, D), x.dtype),
        compiler_params=pltpu.CompilerParams(
            dimension_semantics=("arbitrary", "arbitrary"),
            vmem_limit_bytes=V7X_VMEM_LIMIT_BYTES),
        name="out_ffn",
    )(x, attn, sgu, mod, mod, mod, mod, w_o, ln1_g, ln1_b, w_gate, w_up, w_down, ln2_g, ln2_b)


def _rope_tables(seq, scale):
    half = ROT_DIM // 2
    inv_freq = np.float32(ROPE_THETA) ** (-np.arange(half, dtype=np.float32) * np.float32(2.0 / ROT_DIM))
    ang = np.arange(seq, dtype=np.float32)[:, None] * inv_freq[None, :]
    cos, sin = np.cos(ang), np.sin(ang)
    pad = np.zeros((seq, DIFF_HEAD_DIM - ROT_DIM), np.float32)
    zer = np.zeros((seq, half), np.float32)
    c_tab = np.concatenate([cos, cos, pad + 1.0], axis=-1)
    s_lo = np.concatenate([-sin, zer, pad], axis=-1)
    s_hi = np.concatenate([zer, sin, pad], axis=-1)
    tabs = np.stack([c_tab, s_lo, s_hi]).astype(np.float32) * np.float32(scale)
    return jnp.asarray(np.tile(tabs, (1, 1, V7X_LANES // DIFF_HEAD_DIM)))


def kernel(x, c, ada_w, ada_b, w_in, lambda_q1, lambda_k1, lambda_q2, lambda_k2, subln_g,
           sgu_ln_g, sgu_ln_b, sgu_w, sgu_b, w_o, ln1_g, ln1_b, w_gate, w_up, w_down,
           ln2_g, ln2_b):
    B, S, D = x.shape
    rope_q = _rope_tables(S, DIFF_HEAD_DIM ** -0.5 * math.log2(math.e))
    rope_k = _rope_tables(S, 1.0)
    for l in range(DEPTH):
        lam_init = 0.8 - 0.6 * math.exp(-0.3 * l)
        mod = _adaln_mod(c, ada_w[l], ada_b[l][None, :])
        (q, k, v, sgu), (w_o16, w_gate16, w_up16, w_down16) = _in_proj(
            x, mod, w_in[l], rope_q, rope_k,
            sgu_ln_g[l][None, :], sgu_ln_b[l][None, :], sgu_w[l], sgu_b[l].T,
            (w_o[l], w_gate[l], w_up[l], w_down[l]))
        attn = _diff_attn(q, k, v, lambda_q1[l][None, :], lambda_k1[l][None, :],
                          lambda_q2[l][None, :], lambda_k2[l][None, :],
                          subln_g[l][None, :], lam_init)
        x = _out_ffn(x, attn, sgu, mod, w_o16, ln1_g[l][None, :], ln1_b[l][None, :],
                     w_gate16, w_up16, w_down16, ln2_g[l][None, :], ln2_b[l][None, :])
    return x
```
